```python
import math
import jax, jax.numpy as jnp
from jax import lax
import numpy as np

D_MODEL = 1024
BATCH = 8
SEQ = 8192
DEPTH = 2
DEC_BATCH = 16
DEC_SEQ = 32
PAST_LEN = 4096

CHUNK = 64
D_SSM = D_MODEL // 2
SSM_GROUP = 16
N_GROUPS = D_SSM // SSM_GROUP
STATE = 64
N_HEADS = 8
HEAD_DIM = 64
D_ATT = N_HEADS * HEAD_DIM
IDX_HEADS = 8
IDX_DIM = 64
TOPK_MAX = 256
N_BRANCH = 2
EPS = 1e-6
DT_MIN = 1e-3
DT_MAX = 1e-1
SPLITS = (D_SSM, D_SSM, D_ATT, D_ATT, D_ATT, D_ATT, IDX_HEADS * IDX_DIM, IDX_DIM, IDX_HEADS, N_BRANCH * D_MODEL)
D_IN = 4 * D_ATT + 2 * D_SSM + IDX_HEADS * IDX_DIM + IDX_DIM + IDX_HEADS + N_BRANCH * D_MODEL

kernel_name = 'hybrid_s5_dsa_stream_step'

f32 = jnp.float32


def rms_norm(x, g):
    xf = x.astype(f32)
    return xf * lax.rsqrt(jnp.mean(xf * xf, axis=-1, keepdims=True) + EPS) * g.astype(f32)


def ssm_discretize(a_re, a_im, log_dt, b_re, b_im):
    a_re = a_re.astype(f32); a_im = a_im.astype(f32)
    dt = jnp.exp(log_dt.astype(f32))[:, None]
    mag = jnp.exp(a_re * dt); ang = a_im * dt
    ab_re = mag * jnp.cos(ang); ab_im = mag * jnp.sin(ang)
    den = a_re * a_re + a_im * a_im
    nr = ab_re - 1.0; ni = ab_im
    f_re = (nr * a_re + ni * a_im) / den
    f_im = (ni * a_re - nr * a_im) / den
    b_re = b_re.astype(f32); b_im = b_im.astype(f32)
    bb_re = f_re[..., None] * b_re - f_im[..., None] * b_im
    bb_im = f_re[..., None] * b_im + f_im[..., None] * b_re
    return ab_re, ab_im, bb_re, bb_im


def _cplx_combine(e1, e2):
    a1r, a1i, b1r, b1i = e1
    a2r, a2i, b2r, b2i = e2
    ar = a1r * a2r - a1i * a2i
    ai = a1r * a2i + a1i * a2r
    br = a2r * b1r - a2i * b1i + b2r
    bi = a2r * b1i + a2i * b1r + b2i
    return ar, ai, br, bi


def ssm_block(u, h_re, h_im, ab_re, ab_im, bb_re, bb_im, c_re, c_im, d_skip):
    bu_re = jnp.einsum('gpn,btgn->btgp', bb_re, u)
    bu_im = jnp.einsum('gpn,btgn->btgp', bb_im, u)
    bu_re = bu_re.at[:, 0].add(ab_re * h_re - ab_im * h_im)
    bu_im = bu_im.at[:, 0].add(ab_re * h_im + ab_im * h_re)
    a_re = jnp.broadcast_to(ab_re, bu_re.shape)
    a_im = jnp.broadcast_to(ab_im, bu_im.shape)
    _, _, hr, hi = lax.associative_scan(_cplx_combine, (a_re, a_im, bu_re, bu_im), axis=1)
    y = (jnp.einsum('gnp,btgp->btgn', c_re, hr) - jnp.einsum('gnp,btgp->btgn', c_im, hi)
         + d_skip * u)
    return y, hr[:, -1], hi[:, -1]


def ssm_scan(u, h_re, h_im, disc, c_re, c_im, d_skip):
    ab_re, ab_im, bb_re, bb_im = disc
    c_re = c_re.astype(f32); c_im = c_im.astype(f32); d_skip = d_skip.astype(f32)
    B_, T = u.shape[:2]
    if T > CHUNK and T % CHUNK == 0:
        n = T // CHUNK
        uc = jnp.moveaxis(u.reshape(B_, n, CHUNK, N_GROUPS, SSM_GROUP), 1, 0)

        def step(carry, u_c):
            hr, hi = carry
            y, hr, hi = ssm_block(u_c, hr, hi, ab_re, ab_im, bb_re, bb_im, c_re, c_im, d_skip)
            return (hr, hi), y

        (hr, hi), ys = lax.scan(step, (h_re, h_im), uc)
        return jnp.moveaxis(ys, 0, 1).reshape(B_, T, N_GROUPS, SSM_GROUP), hr, hi
    return ssm_block(u, h_re, h_im, ab_re, ab_im, bb_re, bb_im, c_re, c_im, d_skip)


def dsa_attend(q, qi, wi, k_all, v_all, ki_all, n_valid, n_sel):
    logits = jnp.einsum('bthd,bsd->bths', qi.astype(f32), ki_all.astype(f32)) * (IDX_DIM ** -0.5)
    score = jnp.einsum('bth,bths->bts', wi.astype(f32), jax.nn.relu(logits))
    S = k_all.shape[1]
    adm = jnp.arange(S) < n_valid
    score = jnp.where(adm[None, None, :], score, -jnp.inf)
    top_val, top_idx = lax.top_k(score, n_sel)
    valid = jnp.isfinite(top_val)
    gather = jax.vmap(lambda rows, idx: rows[idx])
    kg = gather(k_all, top_idx).astype(f32)
    vg = gather(v_all, top_idx).astype(f32)
    s = jnp.einsum('bthd,btkhd->bthk', q.astype(f32), kg) * (HEAD_DIM ** -0.5)
    s = jnp.where(valid[:, :, None, :], s, -jnp.inf)
    p = jax.nn.softmax(s, axis=-1)
    return jnp.einsum('bthk,btkhd->bthd', p, vg)


def layer(x, c, w_mod, b_mod, g_norm, w_in, a_re, a_im, log_dt, b_re, b_im, c_re, c_im, d_skip,
          w_glu, b_glu, w_ps, w_pa, w_o, h_re, h_im, past_k, past_v, past_ki):
    dtype = x.dtype
    B_, T = x.shape[:2]
    mod = jax.nn.silu(c.astype(f32)) @ w_mod.astype(f32) + b_mod.astype(f32)
    shift, scale, gate = jnp.split(mod, 3, axis=-1)
    h = rms_norm(x, g_norm) * (1.0 + scale[:, None]) + shift[:, None]
    z = jnp.einsum('btd,de->bte', h, w_in.astype(f32))
    offs = list(np.cumsum(SPLITS)[:-1])
    u, zs, q, k, v, za, qi, ki, wi, gm = jnp.split(z, offs, axis=-1)

    disc = ssm_discretize(a_re, a_im, log_dt, b_re, b_im)
    us = u.reshape(B_, T, N_GROUPS, SSM_GROUP)
    ys, hr, hi = ssm_scan(us, h_re.astype(f32), h_im.astype(f32), disc, c_re, c_im, d_skip)
    ys = jax.nn.gelu(ys.reshape(B_, T, D_SSM))
    ys = ys * jax.nn.sigmoid(ys @ w_glu.astype(f32) + b_glu.astype(f32))
    ys = ys * jax.nn.silu(zs)

    q = q.reshape(B_, T, N_HEADS, HEAD_DIM)
    k = k.reshape(B_, T, N_HEADS, HEAD_DIM)
    v = v.reshape(B_, T, N_HEADS, HEAD_DIM)
    qi = qi.reshape(B_, T, IDX_HEADS, IDX_DIM)
    wi = wi * (IDX_HEADS ** -0.5)
    if past_k is None:
        n_sel = min(TOPK_MAX, T // 4)

        def one_chunk(ci):
            q0 = ci * CHUNK
            sl = lambda a: lax.dynamic_slice_in_dim(a, q0, CHUNK, axis=1)
            return dsa_attend(sl(q), sl(qi), sl(wi), k, v, ki, q0 + CHUNK, n_sel)

        o = lax.map(one_chunk, jnp.arange(T // CHUNK))
        o = jnp.moveaxis(o, 0, 1).reshape(B_, T, N_HEADS, HEAD_DIM)
    else:
        k_all = jnp.concatenate([past_k.astype(f32), k], axis=1)
        v_all = jnp.concatenate([past_v.astype(f32), v], axis=1)
        ki_all = jnp.concatenate([past_ki.astype(f32), ki], axis=1)
        L = k_all.shape[1]
        o = dsa_attend(q, qi, wi, k_all, v_all, ki_all, L, min(TOPK_MAX, L // 4))
    ya = o.reshape(B_, T, D_ATT) * jax.nn.silu(za)

    g_s, g_a = jnp.split(gm, N_BRANCH, axis=-1)
    merged = (jax.nn.sigmoid(g_s) * (ys @ w_ps.astype(f32))
              + jax.nn.sigmoid(g_a) * (ya @ w_pa.astype(f32)))
    out = merged @ w_o.astype(f32)
    x_new = (x.astype(f32) + gate[:, None] * out).astype(dtype)
    return x_new, (k.astype(dtype), v.astype(dtype), ki.astype(dtype), hr, hi)


def setup_inputs(seed: int = 0) -> dict:
    key = jax.random.key(seed)
    ks = jax.random.split(key, 32)
    nrm = lambda i, shape, s: jax.random.normal(ks[i], shape, f32) * s
    n_idx = jnp.arange(STATE, dtype=f32)
    return {
        'x_prompt': nrm(0, (BATCH, SEQ, D_MODEL), 1.0),
        'x_sample': nrm(1, (DEC_BATCH, DEC_SEQ, D_MODEL), 1.0),
        'cache_k': nrm(2, (DEPTH, DEC_BATCH, PAST_LEN, N_HEADS, HEAD_DIM), 1.0),
        'cache_v': nrm(3, (DEPTH, DEC_BATCH, PAST_LEN, N_HEADS, HEAD_DIM), 1.0),
        'cache_kidx': nrm(4, (DEPTH, DEC_BATCH, PAST_LEN, IDX_DIM), 1.0),
        'state_ssm_re': nrm(5, (DEPTH, DEC_BATCH, N_GROUPS, STATE), 0.5),
        'state_ssm_im': nrm(6, (DEPTH, DEC_BATCH, N_GROUPS, STATE), 0.5),
        'c_prompt': nrm(7, (BATCH, D_MODEL), 1.0),
        'c_sample': nrm(8, (DEC_BATCH, D_MODEL), 1.0),
        'w_mod': nrm(9, (DEPTH, D_MODEL, 3 * D_MODEL), 0.5 * D_MODEL ** -0.5),
        'b_mod': nrm(10, (DEPTH, 3 * D_MODEL), 0.01),
        'g_norm': 1.0 + nrm(11, (DEPTH, D_MODEL), 0.01),
        'w_in': nrm(12, (DEPTH, D_MODEL, D_IN), D_MODEL ** -0.5),
        'a_re': -0.5 + nrm(13, (DEPTH, N_GROUPS, STATE), 0.01),
        'a_im': math.pi * n_idx + nrm(14, (DEPTH, N_GROUPS, STATE), 0.01),
        'log_dt': jax.random.uniform(ks[15], (DEPTH, N_GROUPS), f32, math.log(DT_MIN), math.log(DT_MAX)),
        'b_re': nrm(16, (DEPTH, N_GROUPS, STATE, SSM_GROUP), (2 * SSM_GROUP) ** -0.5),
        'b_im': nrm(17, (DEPTH, N_GROUPS, STATE, SSM_GROUP), (2 * SSM_GROUP) ** -0.5),
        'c_re': nrm(18, (DEPTH, N_GROUPS, SSM_GROUP, STATE), STATE ** -0.5),
        'c_im': nrm(19, (DEPTH, N_GROUPS, SSM_GROUP, STATE), STATE ** -0.5),
        'd_skip': nrm(20, (DEPTH, N_GROUPS, SSM_GROUP), 1.0),
        'w_glu': nrm(21, (DEPTH, D_SSM, D_SSM), D_SSM ** -0.5),
        'b_glu': nrm(22, (DEPTH, D_SSM), 0.01),
        'w_ps': nrm(23, (DEPTH, D_SSM, D_MODEL), D_SSM ** -0.5),
        'w_pa': nrm(24, (DEPTH, D_ATT, D_MODEL), D_ATT ** -0.5),
        'w_o': nrm(25, (DEPTH, D_MODEL, D_MODEL), D_MODEL ** -0.5),
        'g_final': 1.0 + nrm(26, (D_MODEL,), 0.01),
    }


def reference(x_prompt, x_sample, cache_k, cache_v, cache_kidx, state_ssm_re, state_ssm_im,
              c_prompt, c_sample, w_mod, b_mod, g_norm, w_in, a_re, a_im, log_dt, b_re, b_im,
              c_re, c_im, d_skip, w_glu, b_glu, w_ps, w_pa, w_o, g_final):
    xp, xs = x_prompt, x_sample
    kp, vp, kip, srp, sip = [], [], [], [], []
    kss, vss, kis, srs, sis = [], [], [], [], []
    zeros = jnp.zeros((x_prompt.shape[0], N_GROUPS, STATE), f32)
    for l in range(DEPTH):
        lw = (w_mod[l], b_mod[l], g_norm[l], w_in[l], a_re[l], a_im[l], log_dt[l], b_re[l], b_im[l],
              c_re[l], c_im[l], d_skip[l], w_glu[l], b_glu[l], w_ps[l], w_pa[l], w_o[l])
        xp, (k1, v1, ki1, hr1, hi1) = layer(xp, c_prompt, *lw, zeros, zeros, None, None, None)
        xs, (k2, v2, ki2, hr2, hi2) = layer(xs, c_sample, *lw, state_ssm_re[l], state_ssm_im[l],
                                            cache_k[l], cache_v[l], cache_kidx[l])
        kp.append(k1); vp.append(v1); kip.append(ki1); srp.append(hr1); sip.append(hi1)
        kss.append(k2); vss.append(v2); kis.append(ki2); srs.append(hr2); sis.append(hi2)
    y_prompt = rms_norm(xp, g_final).astype(x_prompt.dtype)
    y_sample = rms_norm(xs, g_final).astype(x_sample.dtype)
    return (y_prompt, y_sample,
            jnp.stack(kp), jnp.stack(vp), jnp.stack(kip), jnp.stack(srp), jnp.stack(sip),
            jnp.stack(kss), jnp.stack(vss), jnp.stack(kis), jnp.stack(srs), jnp.stack(sis))
```

```python
import functools
import math

import jax
import jax.numpy as jnp
from jax import lax
from jax.experimental import pallas as pl
from jax.experimental.pallas import tpu as pltpu

F32 = jnp.float32
I32 = jnp.int32
MXU_DTYPE = jnp.bfloat16

CHUNK = 64
SSM_GROUP = 16
STATE = 64
N_HEADS = 8
HEAD_DIM = 64
IDX_HEADS = 8
IDX_DIM = 64
TOPK_MAX = 256
EPS = 1e-6

LANES = 128
SUBLANES = 8
V7X_VMEM_BYTES = 64 * 1024 * 1024
VMEM_LIMIT = V7X_VMEM_BYTES - 8 * 1024 * 1024

INT_MIN = -(2 ** 31)
MASK_NEG = -1e30
M_INIT = -1e29
NO_TIE_LIMIT = 2 ** 30


def _cparams(sem):
    return pltpu.CompilerParams(dimension_semantics=sem, vmem_limit_bytes=VMEM_LIMIT)


def _silu(x):
    return x * jax.nn.sigmoid(x)


def _mxu_dot(a, b):
    return jnp.dot(a.astype(MXU_DTYPE), b.astype(MXU_DTYPE), preferred_element_type=F32)


def _split_hi_lo(a):
    hi = a.astype(jnp.bfloat16)
    lo = (a - hi.astype(F32)).astype(jnp.bfloat16)
    return hi, lo


def _dot_3pass(a, b):
    ah, al = _split_hi_lo(a)
    bh, bl = _split_hi_lo(b)
    d = functools.partial(jnp.dot, preferred_element_type=F32)
    return d(ah, bh) + (d(ah, bl) + d(al, bh))


def _mod_kernel(c_ref, w_ref, b_ref, o_ref):
    o_ref[...] = _dot_3pass(_silu(c_ref[...]), w_ref[...]) + b_ref[...]


def _modulation(c_all, w_mod, b_mod):
    depth, d, d3 = w_mod.shape
    rows = c_all.shape[0]
    cols = d3 // 4
    return pl.pallas_call(
        _mod_kernel,
        grid=(depth, d3 // cols),
        in_specs=[
            pl.BlockSpec((rows, d), lambda l, j: (0, 0)),
            pl.BlockSpec((None, d, cols), lambda l, j: (l, 0, j)),
            pl.BlockSpec((None, 1, cols), lambda l, j: (l, 0, j)),
        ],
        out_specs=pl.BlockSpec((None, rows, cols), lambda l, j: (l, 0, j)),
        out_shape=jax.ShapeDtypeStruct((depth, rows, d3), F32),
        compiler_params=_cparams(("arbitrary", "arbitrary")),
        name="adaln_mod",
    )(c_all, w_mod, b_mod.reshape(depth, 1, d3))


def _ssm_prep_kernel(are_ref, aim_ref, ldt_ref, bre_ref, bim_ref, cre_ref, cim_ref,
                     wb_ref, wc_ref, coef_ref):
    a_re = are_ref[...]
    a_im = aim_ref[...]
    dt = jnp.exp(ldt_ref[...])
    gp = a_re.shape[-1]

    def power(k):
        mag = jnp.exp(a_re * dt * k)
        ang = a_im * dt * k
        return mag * jnp.cos(ang), mag * jnp.sin(ang)

    ab_re, ab_im = power(1.0)
    den = a_re * a_re + a_im * a_im
    nr = ab_re - 1.0
    ni = ab_im
    f_re = (nr * a_re + ni * a_im) / den
    f_im = (ni * a_re - nr * a_im) / den

    rows_b = lax.broadcasted_iota(I32, bre_ref.shape, 0) // SSM_GROUP
    cols_b = lax.broadcasted_iota(I32, bre_ref.shape, 1) // STATE
    on_b = rows_b == cols_b
    b_re = bre_ref[...]
    b_im = bim_ref[...]
    wb_ref[:, 0:gp] = jnp.where(on_b, f_re * b_re - f_im * b_im, 0.0).astype(wb_ref.dtype)
    wb_ref[:, gp:2 * gp] = jnp.where(on_b, f_re * b_im + f_im * b_re, 0.0).astype(wb_ref.dtype)

    rows_c = lax.broadcasted_iota(I32, cre_ref.shape, 0) // STATE
    cols_c = lax.broadcasted_iota(I32, cre_ref.shape, 1) // SSM_GROUP
    on_c = rows_c == cols_c
    wc_ref[0:gp, :] = jnp.where(on_c, cre_ref[...], 0.0).astype(wc_ref.dtype)
    wc_ref[gp:2 * gp, :] = jnp.where(on_c, -cim_ref[...], 0.0).astype(wc_ref.dtype)

    row = lax.broadcasted_iota(I32, (SUBLANES, gp), 0)
    for n, d in enumerate((1, 2, 4)):
        p_re, p_im = power(float(d))
        coef_ref[2 * n] = jnp.where(row >= d, p_re, 0.0)
        coef_ref[2 * n + 1] = jnp.where(row >= d, p_im, 0.0)
    c_re, c_im = power((row + 1).astype(F32))
    coef_ref[6] = c_re
    coef_ref[7] = c_im


def _ssm_prep(a_re, a_im, log_dt, b_re, b_im, c_re, c_im):
    g, p = a_re.shape
    n = b_re.shape[-1]
    gp, gn = g * p, g * n
    row = lambda a: a.reshape(1, gp)
    ldt = jnp.broadcast_to(log_dt[:, None], (g, p)).reshape(1, gp)
    b_exp = lambda b: jnp.broadcast_to(b.transpose(2, 0, 1).reshape(1, n, gp), (g, n, gp)).reshape(gn, gp)
    c_exp = lambda c: jnp.broadcast_to(c.transpose(0, 2, 1).reshape(gp, 1, n), (gp, g, n)).reshape(gp, gn)
    return pl.pallas_call(
        _ssm_prep_kernel,
        out_shape=(jax.ShapeDtypeStruct((gn, 2 * gp), MXU_DTYPE),
                   jax.ShapeDtypeStruct((2 * gp, gn), MXU_DTYPE),
                   jax.ShapeDtypeStruct((8, SUBLANES, gp), F32)),
        compiler_params=pltpu.CompilerParams(vmem_limit_bytes=VMEM_LIMIT),
        name="ssm_prep",
    )(row(a_re), row(a_im), ldt, b_exp(b_re), b_exp(b_im), c_exp(c_re), c_exp(c_im))


_MAIN_SEGS = ("u", "zs", "za", "gs", "ga", "k", "v", "q", "qi")


def _main_offsets(d):
    half = d // 2
    widths = dict(u=half, zs=half, za=half, gs=d, ga=d, k=half, v=half, q=half, qi=half)
    offs, o = {}, 0
    for name in _MAIN_SEGS:
        offs[name] = (o, o + widths[name])
        o += widths[name]
    return offs, o


def _inproj_kernel(x_ref, shift_ref, scale_ref, g_ref, wmain_ref, wsmall_ref,
                   u_ref, zs_ref, za_ref, gs_ref, ga_ref, k_ref, v_ref, ki_ref, wi_ref,
                   qhm_ref, qihm_ref, vhm_ref, kt_ref, kit_ref, *, offs):
    x = x_ref[0]
    h = x * lax.rsqrt(jnp.mean(x * x, axis=-1, keepdims=True) + EPS) * g_ref[...]
    h = h * (1.0 + scale_ref[0]) + shift_ref[0]
    hb = h.astype(MXU_DTYPE)

    def proj(name):
        c0, c1 = offs[name]
        return jnp.dot(hb, wmain_ref[:, c0:c1], preferred_element_type=F32)

    u_ref[0] = proj("u")
    zs_ref[0] = proj("zs")
    za_ref[0] = proj("za")
    gs_ref[0] = proj("gs")
    ga_ref[0] = proj("ga")

    def head_major(z, ref):
        for hh in range(N_HEADS):
            ref[0, hh] = z[:, hh * HEAD_DIM:(hh + 1) * HEAD_DIM].astype(ref.dtype)

    zk = proj("k")
    k_ref[0] = zk
    kt_ref[0] = zk.T.astype(kt_ref.dtype)
    zv = proj("v")
    v_ref[0] = zv
    head_major(zv, vhm_ref)
    head_major(proj("q"), qhm_ref)
    head_major(proj("qi"), qihm_ref)

    zsm = jnp.dot(hb, wsmall_ref[...], preferred_element_type=F32)
    ki_ref[0] = zsm[:, 0:IDX_DIM]
    wi_ref[0] = zsm * (IDX_HEADS ** -0.5)
    kit_ref[0] = zsm.T[0:IDX_DIM, :].astype(kit_ref.dtype)


def _inproj_weights(w_in, d):
    half = d // 2
    idx_w = IDX_HEADS * IDX_DIM
    o = 0
    seg = {}
    for name, width in (("u", half), ("zs", half), ("q", half), ("k", half), ("v", half), ("za", half),
                        ("qi", idx_w), ("ki", IDX_DIM), ("wi", IDX_HEADS), ("gs", d), ("ga", d)):
        seg[name] = w_in[:, o:o + width]
        o += width
    seg["q"] = seg["q"] * (HEAD_DIM ** -0.5)
    seg["qi"] = seg["qi"] * (IDX_DIM ** -0.5)
    wmain = jnp.concatenate([seg[n] for n in _MAIN_SEGS], axis=1).astype(MXU_DTYPE)
    wsmall = jnp.concatenate(
        [seg["ki"], seg["wi"], jnp.zeros((d, LANES - IDX_DIM - IDX_HEADS), w_in.dtype)], axis=1).astype(MXU_DTYPE)
    return wmain, wsmall


def _inproj(x, shift, scale, g_norm, wmain, wsmall, tm):
    b, t, d = x.shape
    half = d // 2
    offs, wcols = _main_offsets(d)
    nt = t // tm
    row = lambda width: pl.BlockSpec((1, tm, width), lambda i, j: (i, j, 0))
    hm = pl.BlockSpec((1, N_HEADS, tm, HEAD_DIM), lambda i, j: (i, 0, j, 0))
    per_b = pl.BlockSpec((1, 1, d), lambda i, j: (i, 0, 0))
    const = lambda shape: pl.BlockSpec(shape, lambda i, j: (0,) * len(shape))
    f = lambda *s: jax.ShapeDtypeStruct(s, F32)
    bf = lambda *s: jax.ShapeDtypeStruct(s, MXU_DTYPE)
    return pl.pallas_call(
        functools.partial(_inproj_kernel, offs=offs),
        grid=(b, nt),
        in_specs=[row(d), per_b, per_b, const((1, d)), const((d, wcols)), const((d, LANES))],
        out_specs=[row(half), row(half), row(half), row(d), row(d), row(half), row(half),
                   row(IDX_DIM), row(LANES), hm, hm, hm,
                   pl.BlockSpec((1, half, tm), lambda i, j: (i, 0, j)),
                   pl.BlockSpec((1, IDX_DIM, tm), lambda i, j: (i, 0, j))],
        out_shape=[f(b, t, half), f(b, t, half), f(b, t, half), f(b, t, d), f(b, t, d),
                   f(b, t, half), f(b, t, half), f(b, t, IDX_DIM), f(b, t, LANES),
                   bf(b, N_HEADS, t, HEAD_DIM), bf(b, N_HEADS, t, HEAD_DIM), bf(b, N_HEADS, t, HEAD_DIM),
                   bf(b, half, t), bf(b, IDX_DIM, t)],
        compiler_params=_cparams(("arbitrary", "arbitrary")),
        name="inproj",
    )(x, shift, scale, g_norm.reshape(1, d), wmain, wsmall)


def _ssm_kernel(u_ref, zs_ref, h0r_ref, h0i_ref, wb_ref, wc_ref, coef_ref, d_ref, wglu_ref, bglu_ref,
                y_ref, hr_ref, hi_ref, bu_s, carry_s, *, tt, gp):
    @pl.when(pl.program_id(1) == 0)
    def _():
        carry_s[0:1, :] = h0r_ref[0]
        carry_s[1:2, :] = h0i_ref[0]

    u = u_ref[0]
    bu_s[...] = jnp.dot(u.astype(MXU_DTYPE), wb_ref[...], preferred_element_type=F32)

    def block(r, carry):
        c_re, c_im = carry
        rows = pl.ds(pl.multiple_of(r * SUBLANES, SUBLANES), SUBLANES)
        h_re = bu_s[rows, 0:gp]
        h_im = bu_s[rows, gp:2 * gp]
        for n, d in enumerate((1, 2, 4)):
            a_re = coef_ref[2 * n]
            a_im = coef_ref[2 * n + 1]
            s_re = pltpu.roll(h_re, d, axis=0)
            s_im = pltpu.roll(h_im, d, axis=0)
            h_re, h_im = (h_re + (a_re * s_re - a_im * s_im),
                          h_im + (a_re * s_im + a_im * s_re))
        p_re = coef_ref[6]
        p_im = coef_ref[7]
        h_re, h_im = (h_re + (p_re * c_re - p_im * c_im),
                      h_im + (p_re * c_im + p_im * c_re))
        bu_s[rows, 0:gp] = h_re
        bu_s[rows, gp:2 * gp] = h_im
        return h_re[SUBLANES - 1:SUBLANES, :], h_im[SUBLANES - 1:SUBLANES, :]

    c_re, c_im = lax.fori_loop(0, tt // SUBLANES, block, (carry_s[0:1, :], carry_s[1:2, :]))
    carry_s[0:1, :] = c_re
    carry_s[1:2, :] = c_im
    hr_ref[0] = c_re
    hi_ref[0] = c_im

    y = jnp.dot(bu_s[...].astype(MXU_DTYPE), wc_ref[...], preferred_element_type=F32) + d_ref[...] * u
    y = jax.nn.gelu(y)
    y = y * jax.nn.sigmoid(_mxu_dot(y, wglu_ref[...]) + bglu_ref[...])
    y_ref[0] = y * _silu(zs_ref[0])


def _ssm(u, zs, h0_re, h0_im, wb, wc, coef, d_skip, w_glu, b_glu, tt):
    b, t, dn = u.shape
    gp = wb.shape[1] // 2
    row = pl.BlockSpec((1, tt, dn), lambda i, j: (i, j, 0))
    st = pl.BlockSpec((1, 1, gp), lambda i, j: (i, 0, 0))
    const = lambda shape: pl.BlockSpec(shape, lambda i, j: (0,) * len(shape))
    y, hr, hi = pl.pallas_call(
        functools.partial(_ssm_kernel, tt=tt, gp=gp),
        grid=(b, t // tt),
        in_specs=[row, row, st, st, const(wb.shape), const(wc.shape), const(coef.shape),
                  const((1, dn)), const(w_glu.shape), const((1, dn))],
        out_specs=[row, st, st],
        out_shape=[jax.ShapeDtypeStruct((b, t, dn), F32),
                   jax.ShapeDtypeStruct((b, 1, gp), F32),
                   jax.ShapeDtypeStruct((b, 1, gp), F32)],
        scratch_shapes=[pltpu.VMEM((tt, 2 * gp), F32), pltpu.VMEM((SUBLANES, gp), F32)],
        compiler_params=_cparams(("arbitrary", "arbitrary")),
        name="ssm_scan",
    )(u, zs, h0_re.reshape(b, 1, gp), h0_im.reshape(b, 1, gp), wb, wc, coef,
      d_skip.reshape(1, dn), w_glu.astype(MXU_DTYPE), b_glu.reshape(1, dn))
    return y, hr, hi


def _num_key_blocks(i, tq, tk, nkb, causal):
    if not causal:
        return nkb
    return ((i + 1) * tq + (tk - 1)) // tk


def _select_kernel(qi_ref, wi_ref, kit_ref, bias_ref, keys_s, jstar_s, *,
                   tq, tk, nkb, causal, n_valid, n_sel):
    i = pl.program_id(1)
    nb = _num_key_blocks(i, tq, tk, nkb, causal)
    if causal:
        row_t = i * tq + lax.broadcasted_iota(I32, (tq, 1), 0)
        limit = (row_t // CHUNK + 1) * CHUNK
    else:
        limit = n_valid
    nl = tk // LANES
    wi = wi_ref[0]

    def col_index(j):
        return j * tk + lax.broadcasted_iota(I32, (tq, tk), 1)

    def score_block(j, c):
        kb = kit_ref[0, j]
        s = jnp.zeros((tq, tk), F32)
        for hh in range(IDX_HEADS):
            lg = jnp.dot(qi_ref[0, hh], kb, preferred_element_type=F32)
            s = s + wi[:, IDX_DIM + hh:IDX_DIM + hh + 1] * jnp.maximum(lg, 0.0)
        bits = pltpu.bitcast(s, I32)
        bits = jnp.where(bits == INT_MIN, 0, bits)
        key = bits ^ ((bits >> 31) & 0x7FFFFFFF)
        keys_s[j] = jnp.where(col_index(j) < limit, key, INT_MIN)
        return c

    lax.fori_loop(0, nb, score_block, 0)

    def count(pred_fns):
        def body(j, accs):
            key = keys_s[j]
            out = []
            for fn, acc in zip(pred_fns, accs):
                for c in range(nl):
                    acc = acc + jnp.where(fn(key[:, c * LANES:(c + 1) * LANES]), 1.0, 0.0)
                out.append(acc)
            return tuple(out)
        accs = lax.fori_loop(0, nb, body, tuple(jnp.zeros((tq, LANES), F32) for _ in pred_fns))
        return [jnp.sum(a, axis=-1, keepdims=True) for a in accs]

    def bit_step(bi, u):
        cand = u | lax.shift_left(jnp.int32(1), 31 - bi)
        thr = jnp.broadcast_to(cand ^ INT_MIN, (tq, LANES))
        (cnt,) = count([lambda kk: kk >= thr])
        return jnp.where(cnt >= n_sel, cand, u)

    u = lax.fori_loop(0, 32, bit_step, jnp.zeros((tq, 1), I32))
    thr1 = jnp.maximum(u ^ INT_MIN, INT_MIN + 1)
    thr = jnp.broadcast_to(thr1, (tq, LANES))
    c_gt, c_ge = count([lambda kk: kk > thr, lambda kk: kk >= thr])
    want = n_sel - c_gt
    tie_rows = c_ge > n_sel

    jstar_s[...] = jnp.full((tq, 1), NO_TIE_LIMIT, I32)

    @pl.when(jnp.max(jnp.where(tie_rows, 1.0, 0.0)) > 0.0)
    def _():
        tri = jnp.where(lax.broadcasted_iota(I32, (tk, tk), 0) <= lax.broadcasted_iota(I32, (tk, tk), 1),
                        1.0, 0.0).astype(jnp.bfloat16)

        def body(j, carry):
            before, below = carry
            eq = jnp.where(keys_s[j] == thr1, 1.0, 0.0).astype(jnp.bfloat16)
            pref = before + jnp.dot(eq, tri, preferred_element_type=F32)
            below = below + jnp.sum(jnp.where(pref < want, 1.0, 0.0), axis=-1, keepdims=True)
            return pref[:, tk - 1:tk], below

        _, below = lax.fori_loop(0, nb, body, (jnp.zeros((tq, 1), F32), jnp.zeros((tq, 1), F32)))
        jstar_s[...] = jnp.where(tie_rows, below.astype(I32), NO_TIE_LIMIT)

    jstar = jstar_s[...]

    def bias_block(j, c):
        key = keys_s[j]
        tie_ok = jnp.where(col_index(j) <= jstar, 0.0, MASK_NEG)
        bias = jnp.where(key > thr1, 0.0, jnp.where(key == thr1, tie_ok, MASK_NEG))
        bias_ref[0, 0, j] = bias.astype(bias_ref.dtype)
        return c

    lax.fori_loop(0, nb, bias_block, 0)

    def masked_block(j, c):
        bias_ref[0, 0, j] = jnp.full((tq, tk), MASK_NEG, bias_ref.dtype)
        return c

    lax.fori_loop(nb, nkb, masked_block, 0)


def _select(qi_hm, wi, kit_blk, tq, causal, n_valid, n_sel):
    b, _, t, _ = qi_hm.shape
    _, nkb, _, tk = kit_blk.shape
    nq = t // tq
    return pl.pallas_call(
        functools.partial(_select_kernel, tq=tq, tk=tk, nkb=nkb, causal=causal, n_valid=n_valid, n_sel=n_sel),
        grid=(b, nq),
        in_specs=[pl.BlockSpec((1, IDX_HEADS, tq, IDX_DIM), lambda i, j: (i, 0, j, 0)),
                  pl.BlockSpec((1, tq, LANES), lambda i, j: (i, j, 0)),
                  pl.BlockSpec((1, nkb, IDX_DIM, tk), lambda i, j: (i, 0, 0, 0))],
        out_specs=pl.BlockSpec((1, 1, nkb, tq, tk), lambda i, j: (i, j, 0, 0, 0)),
        out_shape=jax.ShapeDtypeStruct((b, nq, nkb, tq, tk), jnp.bfloat16),
        scratch_shapes=[pltpu.VMEM((nkb, tq, tk), I32), pltpu.VMEM((tq, 1), I32)],
        compiler_params=_cparams(("arbitrary", "arbitrary")),
        name="topk_select",
    )(qi_hm, wi, kit_blk)


def _attn_kernel(q_ref, kt_ref, v_ref, bias_ref, o_ref, m_s, l_s, acc_s, *, tq, tk, nkb, causal):
    i = pl.program_id(1)
    j = pl.program_id(2)

    @pl.when(j == 0)
    def _():
        m_s[...] = jnp.full(m_s.shape, M_INIT, F32)
        l_s[...] = jnp.zeros(l_s.shape, F32)
        acc_s[...] = jnp.zeros(acc_s.shape, F32)

    @pl.when(j < _num_key_blocks(i, tq, tk, nkb, causal))
    def _():
        bias = bias_ref[0, 0, 0].astype(F32)
        for hh in range(N_HEADS):
            s = jnp.dot(q_ref[0, hh], kt_ref[0, hh * HEAD_DIM:(hh + 1) * HEAD_DIM, :],
                        preferred_element_type=F32) + bias
            m_old = m_s[hh]
            m_new = jnp.maximum(m_old, jnp.max(s, axis=-1, keepdims=True))
            alpha = jnp.exp(m_old - m_new)
            p = jnp.exp(s - m_new)
            l_s[hh] = alpha * l_s[hh] + jnp.sum(p, axis=-1, keepdims=True)
            acc_s[hh] = alpha * acc_s[hh] + jnp.dot(p.astype(MXU_DTYPE), v_ref[0, hh],
                                                    preferred_element_type=F32)
            m_s[hh] = m_new

    @pl.when(j == nkb - 1)
    def _():
        o_ref[0] = jnp.concatenate([acc_s[hh] / l_s[hh] for hh in range(N_HEADS)], axis=-1)


def _attention(q_hm, kt, v_hm, bias, tq, tk, causal):
    b, _, t, _ = q_hm.shape
    s_len = kt.shape[-1]
    nq, nkb = t // tq, s_len // tk

    def kblk(i, j):
        return jnp.minimum(j, _num_key_blocks(i, tq, tk, nkb, causal) - 1)

    return pl.pallas_call(
        functools.partial(_attn_kernel, tq=tq, tk=tk, nkb=nkb, causal=causal),
        grid=(b, nq, nkb),
        in_specs=[pl.BlockSpec((1, N_HEADS, tq, HEAD_DIM), lambda bb, i, j: (bb, 0, i, 0)),
                  pl.BlockSpec((1, N_HEADS * HEAD_DIM, tk), lambda bb, i, j: (bb, 0, kblk(i, j))),
                  pl.BlockSpec((1, N_HEADS, tk, HEAD_DIM), lambda bb, i, j: (bb, 0, kblk(i, j), 0)),
                  pl.BlockSpec((1, 1, 1, tq, tk), lambda bb, i, j: (bb, i, kblk(i, j), 0, 0))],
        out_specs=pl.BlockSpec((1, tq, N_HEADS * HEAD_DIM), lambda bb, i, j: (bb, i, 0)),
        out_shape=jax.ShapeDtypeStruct((b, t, N_HEADS * HEAD_DIM), F32),
        scratch_shapes=[pltpu.VMEM((N_HEADS, tq, 1), F32), pltpu.VMEM((N_HEADS, tq, 1), F32),
                        pltpu.VMEM((N_HEADS, tq, HEAD_DIM), F32)],
        compiler_params=_cparams(("arbitrary", "arbitrary", "arbitrary")),
        name="masked_attention",
    )(q_hm, kt, v_hm, bias)


def _merge_kernel(x_ref, ys_ref, att_ref, za_ref, gs_ref, ga_ref, gate_ref, wps_ref, wpa_ref, wo_ref,
                  gfin_ref, o_ref, *, final_norm):
    ya = att_ref[0] * _silu(za_ref[0])
    merged = (jax.nn.sigmoid(gs_ref[0]) * _mxu_dot(ys_ref[0], wps_ref[...])
              + jax.nn.sigmoid(ga_ref[0]) * _mxu_dot(ya, wpa_ref[...]))
    x_new = x_ref[0] + gate_ref[0] * _mxu_dot(merged, wo_ref[...])
    if final_norm:
        x_new = x_new * lax.rsqrt(jnp.mean(x_new * x_new, axis=-1, keepdims=True) + EPS) * gfin_ref[...]
    o_ref[0] = x_new


def _merge(x, ys, att, za, gs, ga, gate, w_ps, w_pa, w_o, g_final, tm, final_norm):
    b, t, d = x.shape
    half = d // 2
    row = lambda width: pl.BlockSpec((1, tm, width), lambda i, j: (i, j, 0))
    const = lambda shape: pl.BlockSpec(shape, lambda i, j: (0,) * len(shape))
    return pl.pallas_call(
        functools.partial(_merge_kernel, final_norm=final_norm),
        grid=(b, t // tm),
        in_specs=[row(d), row(half), row(half), row(half), row(d), row(d),
                  pl.BlockSpec((1, 1, d), lambda i, j: (i, 0, 0)),
                  const((half, d)), const((half, d)), const((d, d)), const((1, d))],
        out_specs=row(d),
        out_shape=jax.ShapeDtypeStruct((b, t, d), F32),
        compiler_params=_cparams(("arbitrary", "arbitrary")),
        name="merge_out",
    )(x, ys, att, za, gs, ga, gate, w_ps.astype(MXU_DTYPE), w_pa.astype(MXU_DTYPE),
      w_o.astype(MXU_DTYPE), g_final.reshape(1, d))


def _row_tile(t, cap):
    tile = min(t, cap)
    assert t % tile == 0, (t, tile)
    return tile


def _key_blocks(s_len):
    s_pad = -(-s_len // LANES) * LANES
    for tk in (512, 384, 256, 128):
        if s_pad % tk == 0:
            return s_pad, tk
    raise AssertionError(s_len)


def _pad_keys(a, s_pad, axis):
    pad = s_pad - a.shape[axis]
    if pad == 0:
        return a
    widths = [(0, 0)] * a.ndim
    widths[axis] = (0, pad)
    return jnp.pad(a, widths)


def _layer(x, mod, lw, ssm_par, h0_re, h0_im, past, final_norm):
    b, t, d = x.shape
    shift, scale, gate = (m.reshape(b, 1, d) for m in jnp.split(mod, 3, axis=-1))
    tm = _row_tile(t, 256)
    (u, zs, za, gs, ga, k, v, ki, wi, q_hm, qi_hm, v_hm, kt, kit) = _inproj(
        x, shift, scale, lw["g_norm"], lw["wmain"], lw["wsmall"], tm)

    wb, wc, coef = ssm_par
    ys, hr, hi = _ssm(u, zs, h0_re, h0_im, wb, wc, coef, lw["d_skip"], lw["w_glu"], lw["b_glu"],
                      _row_tile(t, 256))

    if past is None:
        causal, n_valid = True, t
        n_sel = min(TOPK_MAX, t // 4)
    else:
        past_k, past_v, past_ki = past
        k_all = jnp.concatenate([past_k.reshape(b, -1, d // 2), k], axis=1)
        v_all = jnp.concatenate([past_v.reshape(b, -1, d // 2), v], axis=1)
        ki_all = jnp.concatenate([past_ki, ki], axis=1)
        causal, n_valid = False, k_all.shape[1]
        n_sel = min(TOPK_MAX, n_valid // 4)
        kt = k_all.transpose(0, 2, 1).astype(MXU_DTYPE)
        kit = ki_all.transpose(0, 2, 1).astype(MXU_DTYPE)
        v_hm = v_all.reshape(b, n_valid, N_HEADS, HEAD_DIM).transpose(0, 2, 1, 3).astype(MXU_DTYPE)
    s_pad, tk = _key_blocks(n_valid)
    kt = _pad_keys(kt, s_pad, 2)
    kit = _pad_keys(kit, s_pad, 2)
    v_hm = _pad_keys(v_hm, s_pad, 2)
    nkb = s_pad // tk
    kit_blk = kit.reshape(b, IDX_DIM, nkb, tk).transpose(0, 2, 1, 3)

    tq = _row_tile(t, 256)
    bias = _select(qi_hm, wi, kit_blk, tq, causal, n_valid, n_sel)
    att = _attention(q_hm, kt, v_hm, bias, tq, tk, causal)

    x_new = _merge(x, ys, att, za, gs, ga, gate, lw["w_ps"], lw["w_pa"], lw["w_o"], lw["g_final"],
                   _row_tile(t, 256), final_norm)
    gshape = (b, -1, STATE)
    return x_new, (k.reshape(b, t, N_HEADS, HEAD_DIM), v.reshape(b, t, N_HEADS, HEAD_DIM), ki,
                   hr.reshape(gshape), hi.reshape(gshape))


def kernel(x_prompt, x_sample, cache_k, cache_v, cache_kidx, state_ssm_re, state_ssm_im, c_prompt, c_sample, w_mod, b_mod, g_norm, w_in, a_re, a_im, log_dt, b_re, b_im, c_re, c_im, d_skip, w_glu, b_glu, w_ps, w_pa, w_o, g_final):
    depth = w_mod.shape[0]
    bp, _, d = x_prompt.shape
    g, p = a_re.shape[1:]
    mod = _modulation(jnp.concatenate([c_prompt, c_sample], axis=0), w_mod, b_mod)
    zeros = jnp.zeros((bp, g * p), F32)
    xp, xs = x_prompt, x_sample
    outs_p, outs_s = [], []
    for l in range(depth):
        wmain, wsmall = _inproj_weights(w_in[l], d)
        lw = dict(g_norm=g_norm[l], wmain=wmain, wsmall=wsmall, d_skip=d_skip[l], w_glu=w_glu[l],
                  b_glu=b_glu[l], w_ps=w_ps[l], w_pa=w_pa[l], w_o=w_o[l], g_final=g_final)
        ssm_par = _ssm_prep(a_re[l], a_im[l], log_dt[l], b_re[l], b_im[l], c_re[l], c_im[l])
        last = l == depth - 1
        xp, op = _layer(xp, mod[l, :bp], lw, ssm_par, zeros, zeros, None, last)
        xs, os_ = _layer(xs, mod[l, bp:], lw, ssm_par, state_ssm_re[l].reshape(-1, g * p),
                         state_ssm_im[l].reshape(-1, g * p), (cache_k[l], cache_v[l], cache_kidx[l]), last)
        outs_p.append(op)
        outs_s.append(os_)
    stack = lambda outs: tuple(jnp.stack(z) for z in zip(*outs))
    return (xp, xs) + stack(outs_p) + stack(outs_s)
```

```python
import functools
import math

import jax
import jax.numpy as jnp
from jax import lax
from jax.experimental import pallas as pl
from jax.experimental.pallas import tpu as pltpu

F32 = jnp.float32
I32 = jnp.int32
MXU_DTYPE = jnp.bfloat16

CHUNK = 64
SSM_GROUP = 16
STATE = 64
N_HEADS = 8
HEAD_DIM = 64
IDX_HEADS = 8
IDX_DIM = 64
TOPK_MAX = 256
EPS = 1e-6
LOG2E = math.log2(math.e)

LANES = 128
SUBLANES = 8
V7X_VMEM_BYTES = 64 * 1024 * 1024
VMEM_LIMIT = V7X_VMEM_BYTES - 8 * 1024 * 1024

INT_MIN = -(2 ** 31)
MASK_NEG = -1e30
M_INIT = -1e29
NO_TIE_LIMIT = 2 ** 30
COUNT_ROWS = 32


def _cparams(sem):
    return pltpu.CompilerParams(dimension_semantics=sem, vmem_limit_bytes=VMEM_LIMIT)


def _silu(x):
    return x * jax.nn.sigmoid(x)


def _mxu_dot(a, b):
    return jnp.dot(a.astype(MXU_DTYPE), b.astype(MXU_DTYPE), preferred_element_type=F32)


def _split_hi_lo(a):
    hi = a.astype(jnp.bfloat16)
    lo = (a - hi.astype(F32)).astype(jnp.bfloat16)
    return hi, lo


def _dot_3pass(a, b):
    ah, al = _split_hi_lo(a)
    bh, bl = _split_hi_lo(b)
    d = functools.partial(jnp.dot, preferred_element_type=F32)
    return d(ah, bh) + (d(ah, bl) + d(al, bh))


def _mod_kernel(c_ref, w_ref, b_ref, o_ref):
    o_ref[...] = _dot_3pass(_silu(c_ref[...]), w_ref[...]) + b_ref[...]


def _modulation(c_all, w_mod, b_mod):
    depth, d, d3 = w_mod.shape
    rows = c_all.shape[0]
    cols = d3 // 4
    return pl.pallas_call(
        _mod_kernel,
        grid=(depth, d3 // cols),
        in_specs=[
            pl.BlockSpec((rows, d), lambda l, j: (0, 0)),
            pl.BlockSpec((None, d, cols), lambda l, j: (l, 0, j)),
            pl.BlockSpec((None, 1, cols), lambda l, j: (l, 0, j)),
        ],
        out_specs=pl.BlockSpec((None, rows, cols), lambda l, j: (l, 0, j)),
        out_shape=jax.ShapeDtypeStruct((depth, rows, d3), F32),
        compiler_params=_cparams(("arbitrary", "arbitrary")),
        name="adaln_mod",
    )(c_all, w_mod, b_mod.reshape(depth, 1, d3))


def _ssm_prep_kernel(are_ref, aim_ref, ldt_ref, bre_ref, bim_ref, cre_ref, cim_ref,
                     wb_ref, wc_ref, coef_ref):
    a_re = are_ref[...]
    a_im = aim_ref[...]
    dt = jnp.exp(ldt_ref[...])
    gp = a_re.shape[-1]

    def power(k):
        mag = jnp.exp(a_re * dt * k)
        ang = a_im * dt * k
        return mag * jnp.cos(ang), mag * jnp.sin(ang)

    ab_re, ab_im = power(1.0)
    den = a_re * a_re + a_im * a_im
    nr = ab_re - 1.0
    ni = ab_im
    f_re = (nr * a_re + ni * a_im) / den
    f_im = (ni * a_re - nr * a_im) / den

    rows_b = lax.broadcasted_iota(I32, bre_ref.shape, 0) // SSM_GROUP
    cols_b = lax.broadcasted_iota(I32, bre_ref.shape, 1) // STATE
    on_b = rows_b == cols_b
    b_re = bre_ref[...]
    b_im = bim_ref[...]
    wb_ref[:, 0:gp] = jnp.where(on_b, f_re * b_re - f_im * b_im, 0.0).astype(wb_ref.dtype)
    wb_ref[:, gp:2 * gp] = jnp.where(on_b, f_re * b_im + f_im * b_re, 0.0).astype(wb_ref.dtype)

    rows_c = lax.broadcasted_iota(I32, cre_ref.shape, 0) // STATE
    cols_c = lax.broadcasted_iota(I32, cre_ref.shape, 1) // SSM_GROUP
    on_c = rows_c == cols_c
    wc_ref[0:gp, :] = jnp.where(on_c, cre_ref[...], 0.0).astype(wc_ref.dtype)
    wc_ref[gp:2 * gp, :] = jnp.where(on_c, -cim_ref[...], 0.0).astype(wc_ref.dtype)

    row = lax.broadcasted_iota(I32, (SUBLANES, gp), 0)
    for n, d in enumerate((1, 2, 4)):
        p_re, p_im = power(float(d))
        coef_ref[2 * n] = jnp.where(row >= d, p_re, 0.0)
        coef_ref[2 * n + 1] = jnp.where(row >= d, p_im, 0.0)
    c_re, c_im = power((row + 1).astype(F32))
    coef_ref[6] = c_re
    coef_ref[7] = c_im


def _ssm_prep(a_re, a_im, log_dt, b_re, b_im, c_re, c_im):
    g, p = a_re.shape
    n = b_re.shape[-1]
    gp, gn = g * p, g * n
    row = lambda a: a.reshape(1, gp)
    ldt = jnp.broadcast_to(log_dt[:, None], (g, p)).reshape(1, gp)
    b_exp = lambda b: jnp.broadcast_to(b.transpose(2, 0, 1).reshape(1, n, gp), (g, n, gp)).reshape(gn, gp)
    c_exp = lambda c: jnp.broadcast_to(c.transpose(0, 2, 1).reshape(gp, 1, n), (gp, g, n)).reshape(gp, gn)
    return pl.pallas_call(
        _ssm_prep_kernel,
        out_shape=(jax.ShapeDtypeStruct((gn, 2 * gp), MXU_DTYPE),
                   jax.ShapeDtypeStruct((2 * gp, gn), MXU_DTYPE),
                   jax.ShapeDtypeStruct((8, SUBLANES, gp), F32)),
        compiler_params=pltpu.CompilerParams(vmem_limit_bytes=VMEM_LIMIT),
        name="ssm_prep",
    )(row(a_re), row(a_im), ldt, b_exp(b_re), b_exp(b_im), c_exp(c_re), c_exp(c_im))


_MAIN_SEGS = ("u", "zs", "za", "gs", "ga", "k", "v", "q", "qi")


def _main_offsets(d):
    half = d // 2
    widths = dict(u=half, zs=half, za=half, gs=d, ga=d, k=half, v=half, q=half, qi=half)
    offs, o = {}, 0
    for name in _MAIN_SEGS:
        offs[name] = (o, o + widths[name])
        o += widths[name]
    return offs, o


def _inproj_kernel(x_ref, shift_ref, scale_ref, g_ref, wmain_ref, wsmall_ref,
                   u_ref, zs_ref, za_ref, gs_ref, ga_ref, k_ref, v_ref, ki_ref,
                   wit_ref, qt_ref, qit_ref, vt_ref, khm_ref, *, offs):
    x = x_ref[0]
    h = x * lax.rsqrt(jnp.mean(x * x, axis=-1, keepdims=True) + EPS) * g_ref[...]
    h = h * (1.0 + scale_ref[0]) + shift_ref[0]
    hb = h.astype(MXU_DTYPE)

    def proj(name):
        c0, c1 = offs[name]
        return jnp.dot(hb, wmain_ref[:, c0:c1], preferred_element_type=F32)

    u_ref[0] = proj("u")
    zs_ref[0] = proj("zs")
    za_ref[0] = proj("za")
    gs_ref[0] = proj("gs")
    ga_ref[0] = proj("ga")

    zk = proj("k")
    k_ref[0] = zk
    for hh in range(N_HEADS):
        khm_ref[0, hh] = zk[:, hh * HEAD_DIM:(hh + 1) * HEAD_DIM].astype(khm_ref.dtype)
    zv = proj("v")
    v_ref[0] = zv
    vt_ref[0] = zv.T.astype(vt_ref.dtype)
    qt_ref[0] = proj("q").T.astype(qt_ref.dtype)
    qit_ref[0] = proj("qi").T.astype(qit_ref.dtype)

    zsm = jnp.dot(hb, wsmall_ref[...], preferred_element_type=F32)
    ki_ref[0] = zsm[:, 0:IDX_DIM]
    wit_ref[0] = zsm.T[IDX_DIM:IDX_DIM + IDX_HEADS, :] * (IDX_HEADS ** -0.5)


def _inproj_weights(w_in, d):
    half = d // 2
    idx_w = IDX_HEADS * IDX_DIM
    o = 0
    seg = {}
    for name, width in (("u", half), ("zs", half), ("q", half), ("k", half), ("v", half), ("za", half),
                        ("qi", idx_w), ("ki", IDX_DIM), ("wi", IDX_HEADS), ("gs", d), ("ga", d)):
        seg[name] = w_in[:, o:o + width]
        o += width
    seg["q"] = seg["q"] * (HEAD_DIM ** -0.5 * LOG2E)
    seg["qi"] = seg["qi"] * (IDX_DIM ** -0.5)
    wmain = jnp.concatenate([seg[n] for n in _MAIN_SEGS], axis=1).astype(MXU_DTYPE)
    wsmall = jnp.concatenate(
        [seg["ki"], seg["wi"], jnp.zeros((d, LANES - IDX_DIM - IDX_HEADS), w_in.dtype)], axis=1).astype(MXU_DTYPE)
    return wmain, wsmall


def _inproj(x, shift, scale, g_norm, wmain, wsmall, tm):
    b, t, d = x.shape
    half = d // 2
    offs, wcols = _main_offsets(d)
    nt = t // tm
    row = lambda width: pl.BlockSpec((1, tm, width), lambda i, j: (i, j, 0))
    hm = pl.BlockSpec((1, N_HEADS, tm, HEAD_DIM), lambda i, j: (i, 0, j, 0))
    tr = lambda rows: pl.BlockSpec((1, rows, tm), lambda i, j: (i, 0, j))
    per_b = pl.BlockSpec((1, 1, d), lambda i, j: (i, 0, 0))
    const = lambda shape: pl.BlockSpec(shape, lambda i, j: (0,) * len(shape))
    f = lambda *s: jax.ShapeDtypeStruct(s, F32)
    bf = lambda *s: jax.ShapeDtypeStruct(s, MXU_DTYPE)
    return pl.pallas_call(
        functools.partial(_inproj_kernel, offs=offs),
        grid=(b, nt),
        in_specs=[row(d), per_b, per_b, const((1, d)), const((d, wcols)), const((d, LANES))],
        out_specs=[row(half), row(half), row(half), row(d), row(d), row(half), row(half),
                   row(IDX_DIM), tr(IDX_HEADS), tr(half), tr(half), tr(half), hm],
        out_shape=[f(b, t, half), f(b, t, half), f(b, t, half), f(b, t, d), f(b, t, d),
                   f(b, t, half), f(b, t, half), f(b, t, IDX_DIM), f(b, IDX_HEADS, t),
                   bf(b, half, t), bf(b, half, t), bf(b, half, t), bf(b, N_HEADS, t, HEAD_DIM)],
        compiler_params=_cparams(("arbitrary", "arbitrary")),
        name="inproj",
    )(x, shift, scale, g_norm.reshape(1, d), wmain, wsmall)


def _ssm_kernel(u_ref, zs_ref, h0r_ref, h0i_ref, wb_ref, wc_ref, coef_ref, d_ref, wglu_ref, bglu_ref,
                y_ref, hr_ref, hi_ref, bu_s, carry_s, *, tt, gp):
    @pl.when(pl.program_id(1) == 0)
    def _():
        carry_s[0:1, :] = h0r_ref[0]
        carry_s[1:2, :] = h0i_ref[0]

    u = u_ref[0]
    bu_s[...] = jnp.dot(u.astype(MXU_DTYPE), wb_ref[...], preferred_element_type=F32)

    def block(r, carry):
        c_re, c_im = carry
        rows = pl.ds(pl.multiple_of(r * SUBLANES, SUBLANES), SUBLANES)
        h_re = bu_s[rows, 0:gp]
        h_im = bu_s[rows, gp:2 * gp]
        for n, d in enumerate((1, 2, 4)):
            a_re = coef_ref[2 * n]
            a_im = coef_ref[2 * n + 1]
            s_re = pltpu.roll(h_re, d, axis=0)
            s_im = pltpu.roll(h_im, d, axis=0)
            h_re, h_im = (h_re + (a_re * s_re - a_im * s_im),
                          h_im + (a_re * s_im + a_im * s_re))
        p_re = coef_ref[6]
        p_im = coef_ref[7]
        h_re, h_im = (h_re + (p_re * c_re - p_im * c_im),
                      h_im + (p_re * c_im + p_im * c_re))
        bu_s[rows, 0:gp] = h_re
        bu_s[rows, gp:2 * gp] = h_im
        return h_re[SUBLANES - 1:SUBLANES, :], h_im[SUBLANES - 1:SUBLANES, :]

    c_re, c_im = lax.fori_loop(0, tt // SUBLANES, block, (carry_s[0:1, :], carry_s[1:2, :]))
    carry_s[0:1, :] = c_re
    carry_s[1:2, :] = c_im
    hr_ref[0] = c_re
    hi_ref[0] = c_im

    y = jnp.dot(bu_s[...].astype(MXU_DTYPE), wc_ref[...], preferred_element_type=F32) + d_ref[...] * u
    y = jax.nn.gelu(y)
    y = y * jax.nn.sigmoid(_mxu_dot(y, wglu_ref[...]) + bglu_ref[...])
    y_ref[0] = y * _silu(zs_ref[0])


def _ssm(u, zs, h0_re, h0_im, wb, wc, coef, d_skip, w_glu, b_glu, tt):
    b, t, dn = u.shape
    gp = wb.shape[1] // 2
    row = pl.BlockSpec((1, tt, dn), lambda i, j: (i, j, 0))
    st = pl.BlockSpec((1, 1, gp), lambda i, j: (i, 0, 0))
    const = lambda shape: pl.BlockSpec(shape, lambda i, j: (0,) * len(shape))
    y, hr, hi = pl.pallas_call(
        functools.partial(_ssm_kernel, tt=tt, gp=gp),
        grid=(b, t // tt),
        in_specs=[row, row, st, st, const(wb.shape), const(wc.shape), const(coef.shape),
                  const((1, dn)), const(w_glu.shape), const((1, dn))],
        out_specs=[row, st, st],
        out_shape=[jax.ShapeDtypeStruct((b, t, dn), F32),
                   jax.ShapeDtypeStruct((b, 1, gp), F32),
                   jax.ShapeDtypeStruct((b, 1, gp), F32)],
        scratch_shapes=[pltpu.VMEM((tt, 2 * gp), F32), pltpu.VMEM((SUBLANES, gp), F32)],
        compiler_params=_cparams(("arbitrary", "arbitrary")),
        name="ssm_scan",
    )(u, zs, h0_re.reshape(b, 1, gp), h0_im.reshape(b, 1, gp), wb, wc, coef,
      d_skip.reshape(1, dn), w_glu.astype(MXU_DTYPE), b_glu.reshape(1, dn))
    return y, hr, hi


def _num_key_blocks(i, tq, tk, nkb, causal):
    if not causal:
        return nkb
    return ((i + 1) * tq + (tk - 1)) // tk


def _select_kernel(qit_ref, wit_ref, ki_ref, bias_ref, keys_s, jstar_s, *,
                   tq, tk, nkb, causal, n_valid, n_sel):
    i = pl.program_id(1)
    nb = _num_key_blocks(i, tq, tk, nkb, causal)
    if causal:
        q_t = i * tq + lax.broadcasted_iota(I32, (1, tq), 1)
        limit = (q_t // CHUNK + 1) * CHUNK
    else:
        limit = n_valid
    wit = wit_ref[0]

    def key_index(j):
        return j * tk + lax.broadcasted_iota(I32, (tk, tq), 0)

    def score_block(j, c):
        kb = ki_ref[0, j]
        s = jnp.zeros((tk, tq), F32)
        for hh in range(IDX_HEADS):
            lg = jnp.dot(kb, qit_ref[0, hh * IDX_DIM:(hh + 1) * IDX_DIM, :], preferred_element_type=F32)
            s = s + wit[hh:hh + 1, :] * jnp.maximum(lg, 0.0)
        bits = pltpu.bitcast(s, I32)
        bits = jnp.where(bits == INT_MIN, 0, bits)
        key = bits ^ ((bits >> 31) & 0x7FFFFFFF)
        keys_s[j] = jnp.where(key_index(j) < limit, key, INT_MIN)
        return c

    lax.fori_loop(0, nb, score_block, 0)

    def count(pred_fns):
        def body(j, accs):
            key = keys_s[j]
            return tuple(acc + jnp.sum(jnp.where(fn(key), 1.0, 0.0).reshape(tk // COUNT_ROWS, COUNT_ROWS, tq),
                                       axis=0)
                         for fn, acc in zip(pred_fns, accs))
        accs = lax.fori_loop(0, nb, body, tuple(jnp.zeros((COUNT_ROWS, tq), F32) for _ in pred_fns))
        return [jnp.sum(a, axis=0, keepdims=True) for a in accs]

    def bit_step(bi, u):
        cand = u | lax.shift_left(jnp.int32(1), 31 - bi)
        thr = cand ^ INT_MIN
        (cnt,) = count([lambda kk: kk >= thr])
        return jnp.where(cnt >= n_sel, cand, u)

    u = lax.fori_loop(0, 32, bit_step, jnp.zeros((1, tq), I32))
    thr = jnp.maximum(u ^ INT_MIN, INT_MIN + 1)
    c_gt, c_ge = count([lambda kk: kk > thr, lambda kk: kk >= thr])
    want = n_sel - c_gt
    tie_q = c_ge > n_sel

    jstar_s[...] = jnp.full((1, tq), NO_TIE_LIMIT, I32)

    @pl.when(jnp.max(jnp.where(tie_q, 1.0, 0.0)) > 0.0)
    def _():
        tri = jnp.where(lax.broadcasted_iota(I32, (tk, tk), 0) >= lax.broadcasted_iota(I32, (tk, tk), 1),
                        1.0, 0.0).astype(jnp.bfloat16)

        def body(j, carry):
            before, below = carry
            eq = jnp.where(keys_s[j] == thr, 1.0, 0.0).astype(jnp.bfloat16)
            pref = before + jnp.dot(tri, eq, preferred_element_type=F32)
            below = below + jnp.sum(jnp.where(pref < want, 1.0, 0.0), axis=0, keepdims=True)
            return pref[tk - 1:tk, :], below

        _, below = lax.fori_loop(0, nb, body, (jnp.zeros((1, tq), F32), jnp.zeros((1, tq), F32)))
        jstar_s[...] = jnp.where(tie_q, below.astype(I32), NO_TIE_LIMIT)

    jstar = jstar_s[...]

    def bias_block(j, c):
        key = keys_s[j]
        tie_ok = jnp.where(key_index(j) <= jstar, 0.0, MASK_NEG)
        bias = jnp.where(key > thr, 0.0, jnp.where(key == thr, tie_ok, MASK_NEG))
        bias_ref[0, 0, j] = bias.astype(bias_ref.dtype)
        return c

    lax.fori_loop(0, nb, bias_block, 0)

    def masked_block(j, c):
        bias_ref[0, 0, j] = jnp.full((tk, tq), MASK_NEG, bias_ref.dtype)
        return c

    lax.fori_loop(nb, nkb, masked_block, 0)


def _select(qit, wit, ki_blk, tq, causal, n_valid, n_sel):
    b, width, t = qit.shape
    _, nkb, tk, _ = ki_blk.shape
    nq = t // tq
    return pl.pallas_call(
        functools.partial(_select_kernel, tq=tq, tk=tk, nkb=nkb, causal=causal, n_valid=n_valid, n_sel=n_sel),
        grid=(b, nq),
        in_specs=[pl.BlockSpec((1, width, tq), lambda i, j: (i, 0, j)),
                  pl.BlockSpec((1, IDX_HEADS, tq), lambda i, j: (i, 0, j)),
                  pl.BlockSpec((1, nkb, tk, IDX_DIM), lambda i, j: (i, 0, 0, 0))],
        out_specs=pl.BlockSpec((1, 1, nkb, tk, tq), lambda i, j: (i, j, 0, 0, 0)),
        out_shape=jax.ShapeDtypeStruct((b, nq, nkb, tk, tq), jnp.bfloat16),
        scratch_shapes=[pltpu.VMEM((nkb, tk, tq), I32), pltpu.VMEM((1, tq), I32)],
        compiler_params=_cparams(("arbitrary", "arbitrary")),
        name="topk_select",
    )(qit, wit, ki_blk)


def _attn_kernel(qtile_ref, kblock_ref, qt_ref, k_ref, vt_ref, bias_ref, o_ref, m_s, l_s, acc_s, s_s, bias_s, *,
                 tq, tk, nkb, causal):
    step = pl.program_id(1)
    j = kblock_ref[step]
    last = _num_key_blocks(qtile_ref[step], tq, tk, nkb, causal) - 1

    @pl.when(j == 0)
    def _():
        m_s[...] = jnp.full(m_s.shape, M_INIT, F32)
        l_s[...] = jnp.zeros(l_s.shape, F32)
        acc_s[...] = jnp.zeros(acc_s.shape, F32)

    sub = min(tk, LANES)
    chunks = [slice(c * sub, (c + 1) * sub) for c in range(tk // sub)]
    m_all = m_s[...]
    l_all = l_s[...]
    bias_s[...] = bias_ref[0, 0, 0].astype(F32)

    def stage_a(hh, keys, mx):
        s = jnp.dot(k_ref[0, hh, keys, :], qt_ref[0, hh * HEAD_DIM:(hh + 1) * HEAD_DIM, :],
                    preferred_element_type=F32) + bias_s[keys, :]
        s_s[hh % 2, keys, :] = s
        return jnp.maximum(mx, jnp.max(s, axis=0, keepdims=True))

    def stage_b(hh, keys, m_new, lsum, pv):
        p = jnp.exp2(s_s[hh % 2, keys, :] - m_new)
        lsum = lsum + jnp.sum(p, axis=0, keepdims=True)
        pv = pv + jnp.dot(vt_ref[0, hh * HEAD_DIM:(hh + 1) * HEAD_DIM, keys], p.astype(MXU_DTYPE),
                          preferred_element_type=F32)
        return lsum, pv

    m_new = m_all[0:1, :]
    for keys in chunks:
        m_new = stage_a(0, keys, m_new)
    m_out, l_out = [], []
    for hh in range(N_HEADS):
        rows = slice(hh * HEAD_DIM, (hh + 1) * HEAD_DIM)
        lsum = jnp.zeros((1, tq), F32)
        pv = jnp.zeros((HEAD_DIM, tq), F32)
        m_next = m_all[hh + 1:hh + 2, :] if hh + 1 < N_HEADS else None
        for keys in chunks:
            if m_next is not None:
                m_next = stage_a(hh + 1, keys, m_next)
            lsum, pv = stage_b(hh, keys, m_new, lsum, pv)
        alpha = jnp.exp2(m_all[hh:hh + 1, :] - m_new)
        l_out.append(alpha * l_all[hh:hh + 1, :] + lsum)
        acc_s[rows, :] = alpha * acc_s[rows, :] + pv
        m_out.append(m_new)
        m_new = m_next
    m_s[...] = jnp.concatenate(m_out, axis=0)
    l_s[...] = jnp.concatenate(l_out, axis=0)

    @pl.when(j == last)
    def _():
        out_t = jnp.concatenate(
            [acc_s[hh * HEAD_DIM:(hh + 1) * HEAD_DIM, :] / l_s[hh:hh + 1, :] for hh in range(N_HEADS)], axis=0)
        o_ref[0] = out_t.T


def _attention(qt, k_hm, vt, bias, tq, tk, causal):
    b, width, t = qt.shape
    s_len = vt.shape[-1]
    nq, nkb = t // tq, s_len // tk
    pairs = [(i, j) for i in range(nq) for j in range(_num_key_blocks(i, tq, tk, nkb, causal))]
    qtile = jnp.asarray([p[0] for p in pairs], I32)
    kblock = jnp.asarray([p[1] for p in pairs], I32)
    grid_spec = pltpu.PrefetchScalarGridSpec(
        num_scalar_prefetch=2,
        grid=(b, len(pairs)),
        in_specs=[pl.BlockSpec((1, width, tq), lambda bb, s, qi, kj: (bb, 0, qi[s])),
                  pl.BlockSpec((1, N_HEADS, tk, HEAD_DIM), lambda bb, s, qi, kj: (bb, 0, kj[s], 0)),
                  pl.BlockSpec((1, width, tk), lambda bb, s, qi, kj: (bb, 0, kj[s])),
                  pl.BlockSpec((1, 1, 1, tk, tq), lambda bb, s, qi, kj: (bb, qi[s], kj[s], 0, 0))],
        out_specs=pl.BlockSpec((1, tq, width), lambda bb, s, qi, kj: (bb, qi[s], 0)),
        scratch_shapes=[pltpu.VMEM((N_HEADS, tq), F32), pltpu.VMEM((N_HEADS, tq), F32),
                        pltpu.VMEM((width, tq), F32), pltpu.VMEM((2, tk, tq), F32),
                        pltpu.VMEM((tk, tq), F32)])
    return pl.pallas_call(
        functools.partial(_attn_kernel, tq=tq, tk=tk, nkb=nkb, causal=causal),
        grid_spec=grid_spec,
        out_shape=jax.ShapeDtypeStruct((b, t, width), F32),
        compiler_params=_cparams(("arbitrary", "arbitrary")),
        name="masked_attention",
    )(qtile, kblock, qt, k_hm, vt, bias)


def _merge_kernel(x_ref, ys_ref, att_ref, za_ref, gs_ref, ga_ref, gate_ref, wps_ref, wpa_ref, wo_ref,
                  gfin_ref, o_ref, *, final_norm):
    ya = att_ref[0] * _silu(za_ref[0])
    merged = (jax.nn.sigmoid(gs_ref[0]) * _mxu_dot(ys_ref[0], wps_ref[...])
              + jax.nn.sigmoid(ga_ref[0]) * _mxu_dot(ya, wpa_ref[...]))
    x_new = x_ref[0] + gate_ref[0] * _mxu_dot(merged, wo_ref[...])
    if final_norm:
        x_new = x_new * lax.rsqrt(jnp.mean(x_new * x_new, axis=-1, keepdims=True) + EPS) * gfin_ref[...]
    o_ref[0] = x_new


def _merge(x, ys, att, za, gs, ga, gate, w_ps, w_pa, w_o, g_final, tm, final_norm):
    b, t, d = x.shape
    half = d // 2
    row = lambda width: pl.BlockSpec((1, tm, width), lambda i, j: (i, j, 0))
    const = lambda shape: pl.BlockSpec(shape, lambda i, j: (0,) * len(shape))
    return pl.pallas_call(
        functools.partial(_merge_kernel, final_norm=final_norm),
        grid=(b, t // tm),
        in_specs=[row(d), row(half), row(half), row(half), row(d), row(d),
                  pl.BlockSpec((1, 1, d), lambda i, j: (i, 0, 0)),
                  const((half, d)), const((half, d)), const((d, d)), const((1, d))],
        out_specs=row(d),
        out_shape=jax.ShapeDtypeStruct((b, t, d), F32),
        compiler_params=_cparams(("arbitrary", "arbitrary")),
        name="merge_out",
    )(x, ys, att, za, gs, ga, gate, w_ps.astype(MXU_DTYPE), w_pa.astype(MXU_DTYPE),
      w_o.astype(MXU_DTYPE), g_final.reshape(1, d))


def _row_tile(t, cap):
    tile = min(t, cap)
    assert t % tile == 0, (t, tile)
    return tile


def _key_blocks(s_len):
    s_pad = -(-s_len // LANES) * LANES
    for tk in (512, 384, 256, 128):
        if s_pad % tk == 0:
            return s_pad, tk
    raise AssertionError(s_len)


def _pad_keys(a, s_pad, axis):
    pad = s_pad - a.shape[axis]
    if pad == 0:
        return a
    widths = [(0, 0)] * a.ndim
    widths[axis] = (0, pad)
    return jnp.pad(a, widths)


def _layer(x, mod, lw, ssm_par, h0_re, h0_im, past, final_norm):
    b, t, d = x.shape
    shift, scale, gate = (m.reshape(b, 1, d) for m in jnp.split(mod, 3, axis=-1))
    tm = _row_tile(t, 256)
    (u, zs, za, gs, ga, k, v, ki, wit, qt, qit, vt, k_hm) = _inproj(
        x, shift, scale, lw["g_norm"], lw["wmain"], lw["wsmall"], tm)

    wb, wc, coef = ssm_par
    ys, hr, hi = _ssm(u, zs, h0_re, h0_im, wb, wc, coef, lw["d_skip"], lw["w_glu"], lw["b_glu"],
                      _row_tile(t, 256))

    ki_bf = ki.astype(MXU_DTYPE)
    if past is None:
        causal, n_valid = True, t
    else:
        past_k, past_v, past_ki = past
        causal, n_valid = False, past_k.shape[1] + t
        k_hm = jnp.concatenate([past_k.transpose(0, 2, 1, 3).astype(MXU_DTYPE), k_hm], axis=2)
        vt = jnp.concatenate([past_v.reshape(b, -1, d // 2).transpose(0, 2, 1).astype(MXU_DTYPE), vt], axis=2)
        ki_bf = jnp.concatenate([past_ki.astype(MXU_DTYPE), ki_bf], axis=1)
    n_sel = min(TOPK_MAX, n_valid // 4)
    s_pad, tk = _key_blocks(n_valid)
    k_hm = _pad_keys(k_hm, s_pad, 2)
    vt = _pad_keys(vt, s_pad, 2)
    ki_blk = _pad_keys(ki_bf, s_pad, 1).reshape(b, s_pad // tk, tk, IDX_DIM)

    tq = _row_tile(t, 256)
    bias = _select(qit, wit, ki_blk, tq, causal, n_valid, n_sel)
    att = _attention(qt, k_hm, vt, bias, tq, tk, causal)

    x_new = _merge(x, ys, att, za, gs, ga, gate, lw["w_ps"], lw["w_pa"], lw["w_o"], lw["g_final"],
                   _row_tile(t, 256), final_norm)
    gshape = (b, -1, STATE)
    return x_new, (k.reshape(b, t, N_HEADS, HEAD_DIM), v.reshape(b, t, N_HEADS, HEAD_DIM), ki,
                   hr.reshape(gshape), hi.reshape(gshape))


def kernel(x_prompt, x_sample, cache_k, cache_v, cache_kidx, state_ssm_re, state_ssm_im, c_prompt, c_sample, w_mod, b_mod, g_norm, w_in, a_re, a_im, log_dt, b_re, b_im, c_re, c_im, d_skip, w_glu, b_glu, w_ps, w_pa, w_o, g_final):
    depth = w_mod.shape[0]
    bp, _, d = x_prompt.shape
    g, p = a_re.shape[1:]
    mod = _modulation(jnp.concatenate([c_prompt, c_sample], axis=0), w_mod, b_mod)
    zeros = jnp.zeros((bp, g * p), F32)
    xp, xs = x_prompt, x_sample
    outs_p, outs_s = [], []
    for l in range(depth):
        wmain, wsmall = _inproj_weights(w_in[l], d)
        lw = dict(g_norm=g_norm[l], wmain=wmain, wsmall=wsmall, d_skip=d_skip[l], w_glu=w_glu[l],
                  b_glu=b_glu[l], w_ps=w_ps[l], w_pa=w_pa[l], w_o=w_o[l], g_final=g_final)
        ssm_par = _ssm_prep(a_re[l], a_im[l], log_dt[l], b_re[l], b_im[l], c_re[l], c_im[l])
        last = l == depth - 1
        xp, op = _layer(xp, mod[l, :bp], lw, ssm_par, zeros, zeros, None, last)
        xs, os_ = _layer(xs, mod[l, bp:], lw, ssm_par, state_ssm_re[l].reshape(-1, g * p),
                         state_ssm_im[l].reshape(-1, g * p), (cache_k[l], cache_v[l], cache_kidx[l]), last)
        outs_p.append(op)
        outs_s.append(os_)
    stack = lambda outs: tuple(jnp.stack(z) for z in zip(*outs))
    return (xp, xs) + stack(outs_p) + stack(outs_s)
```

```python
import functools
import math

import jax
import jax.numpy as jnp
from jax import lax
from jax.experimental import pallas as pl
from jax.experimental.pallas import tpu as pltpu

F32 = jnp.float32
I32 = jnp.int32
MXU_DTYPE = jnp.bfloat16

CHUNK = 64
SSM_GROUP = 16
STATE = 64
N_HEADS = 8
HEAD_DIM = 64
IDX_HEADS = 8
IDX_DIM = 64
TOPK_MAX = 256
EPS = 1e-6
LOG2E = math.log2(math.e)

LANES = 128
SUBLANES = 8
V7X_VMEM_BYTES = 64 * 1024 * 1024
VMEM_LIMIT = V7X_VMEM_BYTES - 8 * 1024 * 1024

INT_MIN = -(2 ** 31)
MASK_NEG = -1e30
M_INIT = -1e29
NO_TIE_LIMIT = 2 ** 30
COUNT_ROWS = 32


def _cparams(sem):
    return pltpu.CompilerParams(dimension_semantics=sem, vmem_limit_bytes=VMEM_LIMIT)


def _silu(x):
    return x * jax.nn.sigmoid(x)


def _mxu_dot(a, b):
    return jnp.dot(a.astype(MXU_DTYPE), b.astype(MXU_DTYPE), preferred_element_type=F32)


def _split_hi_lo(a):
    hi = a.astype(jnp.bfloat16)
    lo = (a - hi.astype(F32)).astype(jnp.bfloat16)
    return hi, lo


def _dot_3pass(a, b):
    ah, al = _split_hi_lo(a)
    bh, bl = _split_hi_lo(b)
    d = functools.partial(jnp.dot, preferred_element_type=F32)
    return d(ah, bh) + (d(ah, bl) + d(al, bh))


def _mod_kernel(c_ref, w_ref, b_ref, o_ref):
    o_ref[...] = _dot_3pass(_silu(c_ref[...]), w_ref[...]) + b_ref[...]


def _modulation(c_all, w_mod, b_mod):
    depth, d, d3 = w_mod.shape
    rows = c_all.shape[0]
    cols = d3 // 4
    return pl.pallas_call(
        _mod_kernel,
        grid=(depth, d3 // cols),
        in_specs=[
            pl.BlockSpec((rows, d), lambda l, j: (0, 0)),
            pl.BlockSpec((None, d, cols), lambda l, j: (l, 0, j)),
            pl.BlockSpec((None, 1, cols), lambda l, j: (l, 0, j)),
        ],
        out_specs=pl.BlockSpec((None, rows, cols), lambda l, j: (l, 0, j)),
        out_shape=jax.ShapeDtypeStruct((depth, rows, d3), F32),
        compiler_params=_cparams(("arbitrary", "arbitrary")),
        name="adaln_mod",
    )(c_all, w_mod, b_mod.reshape(depth, 1, d3))


def _ssm_prep_kernel(are_ref, aim_ref, ldt_ref, bre_ref, bim_ref, cre_ref, cim_ref,
                     wb_ref, wc_ref, coef_ref):
    a_re = are_ref[...]
    a_im = aim_ref[...]
    dt = jnp.exp(ldt_ref[...])
    gp = a_re.shape[-1]

    mag = jnp.exp(a_re * dt)
    ang = a_im * dt
    ab_re, ab_im = mag * jnp.cos(ang), mag * jnp.sin(ang)
    den = a_re * a_re + a_im * a_im
    nr = ab_re - 1.0
    ni = ab_im
    f_re = (nr * a_re + ni * a_im) / den
    f_im = (ni * a_re - nr * a_im) / den

    rows_b = lax.broadcasted_iota(I32, bre_ref.shape, 0) // SSM_GROUP
    cols_b = lax.broadcasted_iota(I32, bre_ref.shape, 1) // STATE
    on_b = rows_b == cols_b
    b_re = bre_ref[...]
    b_im = bim_ref[...]
    wb_ref[:, 0:gp] = jnp.where(on_b, f_re * b_re - f_im * b_im, 0.0).astype(wb_ref.dtype)
    wb_ref[:, gp:2 * gp] = jnp.where(on_b, f_re * b_im + f_im * b_re, 0.0).astype(wb_ref.dtype)

    rows_c = lax.broadcasted_iota(I32, cre_ref.shape, 0) // STATE
    cols_c = lax.broadcasted_iota(I32, cre_ref.shape, 1) // SSM_GROUP
    on_c = rows_c == cols_c
    wc_ref[0:gp, :] = jnp.where(on_c, cre_ref[...], 0.0).astype(wc_ref.dtype)
    wc_ref[gp:2 * gp, :] = jnp.where(on_c, -cim_ref[...], 0.0).astype(wc_ref.dtype)

    coef_ref[0] = jnp.broadcast_to(ab_re, (SUBLANES, gp))
    coef_ref[1] = jnp.broadcast_to(ab_im, (SUBLANES, gp))


def _ssm_prep(a_re, a_im, log_dt, b_re, b_im, c_re, c_im):
    g, p = a_re.shape
    n = b_re.shape[-1]
    gp, gn = g * p, g * n
    row = lambda a: a.reshape(1, gp)
    ldt = jnp.broadcast_to(log_dt[:, None], (g, p)).reshape(1, gp)
    b_exp = lambda b: jnp.broadcast_to(b.transpose(2, 0, 1).reshape(1, n, gp), (g, n, gp)).reshape(gn, gp)
    c_exp = lambda c: jnp.broadcast_to(c.transpose(0, 2, 1).reshape(gp, 1, n), (gp, g, n)).reshape(gp, gn)
    return pl.pallas_call(
        _ssm_prep_kernel,
        out_shape=(jax.ShapeDtypeStruct((gn, 2 * gp), MXU_DTYPE),
                   jax.ShapeDtypeStruct((2 * gp, gn), MXU_DTYPE),
                   jax.ShapeDtypeStruct((2, SUBLANES, gp), F32)),
        compiler_params=pltpu.CompilerParams(vmem_limit_bytes=VMEM_LIMIT),
        name="ssm_prep",
    )(row(a_re), row(a_im), ldt, b_exp(b_re), b_exp(b_im), c_exp(c_re), c_exp(c_im))


_MAIN_SEGS = ("u", "zs", "za", "gs", "ga", "k", "v", "q", "qi")


def _main_offsets(d):
    half = d // 2
    widths = dict(u=half, zs=half, za=half, gs=d, ga=d, k=half, v=half, q=half, qi=half)
    offs, o = {}, 0
    for name in _MAIN_SEGS:
        offs[name] = (o, o + widths[name])
        o += widths[name]
    return offs, o


def _inproj_kernel(x_ref, shift_ref, scale_ref, g_ref, wmain_ref, wsmall_ref,
                   u_ref, zs_ref, za_ref, gs_ref, ga_ref, k_ref, v_ref, ki_ref,
                   wit_ref, qt_ref, qit_ref, vt_ref, khm_ref, *, offs):
    x = x_ref[0]
    h = x * lax.rsqrt(jnp.mean(x * x, axis=-1, keepdims=True) + EPS) * g_ref[...]
    h = h * (1.0 + scale_ref[0]) + shift_ref[0]
    hb = h.astype(MXU_DTYPE)

    def proj(name):
        c0, c1 = offs[name]
        return jnp.dot(hb, wmain_ref[:, c0:c1], preferred_element_type=F32)

    u_ref[...] = proj("u")
    zs_ref[...] = proj("zs")
    za_ref[0] = proj("za")
    gs_ref[0] = proj("gs")
    ga_ref[0] = proj("ga")

    zk = proj("k")
    k_ref[0] = zk
    for hh in range(N_HEADS):
        khm_ref[0, hh] = zk[:, hh * HEAD_DIM:(hh + 1) * HEAD_DIM].astype(khm_ref.dtype)
    zv = proj("v")
    v_ref[0] = zv
    vt_ref[0] = zv.T.astype(vt_ref.dtype)
    qt_ref[0] = proj("q").T.astype(qt_ref.dtype)
    qit_ref[0] = proj("qi").T.astype(qit_ref.dtype)

    zsm = jnp.dot(hb, wsmall_ref[...], preferred_element_type=F32)
    ki_ref[0] = zsm[:, 0:IDX_DIM]
    wit_ref[0] = zsm.T[IDX_DIM:IDX_DIM + IDX_HEADS, :] * (IDX_HEADS ** -0.5)


def _inproj_weights(w_in, d):
    half = d // 2
    idx_w = IDX_HEADS * IDX_DIM
    o = 0
    seg = {}
    for name, width in (("u", half), ("zs", half), ("q", half), ("k", half), ("v", half), ("za", half),
                        ("qi", idx_w), ("ki", IDX_DIM), ("wi", IDX_HEADS), ("gs", d), ("ga", d)):
        seg[name] = w_in[:, o:o + width]
        o += width
    seg["q"] = seg["q"] * (HEAD_DIM ** -0.5 * LOG2E)
    seg["qi"] = seg["qi"] * (IDX_DIM ** -0.5)
    wmain = jnp.concatenate([seg[n] for n in _MAIN_SEGS], axis=1).astype(MXU_DTYPE)
    wsmall = jnp.concatenate(
        [seg["ki"], seg["wi"], jnp.zeros((d, LANES - IDX_DIM - IDX_HEADS), w_in.dtype)], axis=1).astype(MXU_DTYPE)
    return wmain, wsmall


def _inproj(x, shift, scale, g_norm, wmain, wsmall, tm):
    b, t, d = x.shape
    half = d // 2
    offs, wcols = _main_offsets(d)
    nt = t // tm
    row = lambda width: pl.BlockSpec((1, tm, width), lambda i, j: (i, j, 0))
    hm = pl.BlockSpec((1, N_HEADS, tm, HEAD_DIM), lambda i, j: (i, 0, j, 0))
    tr = lambda rows: pl.BlockSpec((1, rows, tm), lambda i, j: (i, 0, j))
    tb = pl.BlockSpec((tm, half), lambda i, j: (j, i))
    per_b = pl.BlockSpec((1, 1, d), lambda i, j: (i, 0, 0))
    const = lambda shape: pl.BlockSpec(shape, lambda i, j: (0,) * len(shape))
    f = lambda *s: jax.ShapeDtypeStruct(s, F32)
    bf = lambda *s: jax.ShapeDtypeStruct(s, MXU_DTYPE)
    return pl.pallas_call(
        functools.partial(_inproj_kernel, offs=offs),
        grid=(b, nt),
        in_specs=[row(d), per_b, per_b, const((1, d)), const((d, wcols)), const((d, LANES))],
        out_specs=[tb, tb, row(half), row(d), row(d), row(half), row(half),
                   row(IDX_DIM), tr(IDX_HEADS), tr(half), tr(half), tr(half), hm],
        out_shape=[f(t, b * half), f(t, b * half), f(b, t, half), f(b, t, d), f(b, t, d),
                   f(b, t, half), f(b, t, half), f(b, t, IDX_DIM), f(b, IDX_HEADS, t),
                   bf(b, half, t), bf(b, half, t), bf(b, half, t), bf(b, N_HEADS, t, HEAD_DIM)],
        compiler_params=_cparams(("arbitrary", "arbitrary")),
        name="inproj",
    )(x, shift, scale, g_norm.reshape(1, d), wmain, wsmall)


def _ssm_kernel(u_ref, zs_ref, h0r_ref, h0i_ref, wb_ref, wc_ref, coef_ref, d_ref, wglu_ref, bglu_ref,
                y_ref, hr_ref, hi_ref, bu_s, *, tt, nb, gp):
    @pl.when(pl.program_id(0) == 0)
    def _():
        hr_ref[...] = h0r_ref[...]
        hi_ref[...] = h0i_ref[...]

    u = u_ref[...]
    ub = u.astype(MXU_DTYPE)
    dn = u.shape[-1]
    slabs = dn // LANES
    sw = gp // slabs
    for j in range(slabs):
        ch = slice(j * LANES, (j + 1) * LANES)
        for part in range(2):
            st = slice(part * gp + j * sw, part * gp + (j + 1) * sw)
            bu_s[:, st] = jnp.dot(ub[:, ch], wb_ref[ch, st], preferred_element_type=F32)

    groups = nb // SUBLANES

    def step(t, carry):
        a_re = coef_ref[0]
        a_im = coef_ref[1]
        out = []
        for g in range(groups):
            h_re, h_im = carry[2 * g], carry[2 * g + 1]
            rows = pl.ds(pl.multiple_of(t * nb + g * SUBLANES, SUBLANES), SUBLANES)
            n_re = (a_re * h_re - a_im * h_im) + bu_s[rows, 0:gp]
            n_im = (a_re * h_im + a_im * h_re) + bu_s[rows, gp:2 * gp]
            bu_s[rows, 0:gp] = n_re
            bu_s[rows, gp:2 * gp] = n_im
            out += [n_re, n_im]
        return tuple(out)

    init = []
    for g in range(groups):
        rows = slice(g * SUBLANES, (g + 1) * SUBLANES)
        init += [hr_ref[rows, :], hi_ref[rows, :]]
    fin = lax.fori_loop(0, tt, step, tuple(init), unroll=2)
    for g in range(groups):
        rows = slice(g * SUBLANES, (g + 1) * SUBLANES)
        hr_ref[rows, :] = fin[2 * g]
        hi_ref[rows, :] = fin[2 * g + 1]

    ys = []
    for j in range(slabs):
        ch = slice(j * LANES, (j + 1) * LANES)
        acc = d_ref[:, ch] * u[:, ch]
        for part in range(2):
            st = slice(part * gp + j * sw, part * gp + (j + 1) * sw)
            acc = acc + jnp.dot(bu_s[:, st].astype(MXU_DTYPE), wc_ref[st, ch], preferred_element_type=F32)
        ys.append(acc)
    y = jax.nn.gelu(jnp.concatenate(ys, axis=-1))
    y = y * jax.nn.sigmoid(_mxu_dot(y, wglu_ref[...]) + bglu_ref[...])
    y_ref[...] = y * _silu(zs_ref[...])


def _ssm(u_tb, zs_tb, nb, h0_re, h0_im, wb, wc, coef, d_skip, w_glu, b_glu, rows_per_step):
    rows, dn = u_tb.shape
    gp = wb.shape[1] // 2
    assert nb % SUBLANES == 0 and rows_per_step % nb == 0 and rows % rows_per_step == 0
    tt = rows_per_step // nb
    row = pl.BlockSpec((rows_per_step, dn), lambda j: (j, 0))
    const = lambda shape: pl.BlockSpec(shape, lambda j: (0,) * len(shape))
    return pl.pallas_call(
        functools.partial(_ssm_kernel, tt=tt, nb=nb, gp=gp),
        grid=(rows // rows_per_step,),
        in_specs=[row, row, const((nb, gp)), const((nb, gp)), const(wb.shape), const(wc.shape),
                  const(coef.shape), const((1, dn)), const(w_glu.shape), const((1, dn))],
        out_specs=[row, const((nb, gp)), const((nb, gp))],
        out_shape=[jax.ShapeDtypeStruct((rows, dn), F32),
                   jax.ShapeDtypeStruct((nb, gp), F32),
                   jax.ShapeDtypeStruct((nb, gp), F32)],
        scratch_shapes=[pltpu.VMEM((rows_per_step, 2 * gp), F32)],
        compiler_params=_cparams(("arbitrary",)),
        name="ssm_scan",
    )(u_tb, zs_tb, h0_re, h0_im, wb, wc, coef,
      d_skip.reshape(1, dn), w_glu.astype(MXU_DTYPE), b_glu.reshape(1, dn))


def _num_key_blocks(i, tq, tk, nkb, causal):
    if not causal:
        return nkb
    return ((i + 1) * tq + (tk - 1)) // tk


def _select_kernel(qit_ref, wit_ref, ki_ref, bias_ref, keys_s, *,
                   tq, tk, nkb, causal, n_valid, n_sel):
    i = pl.program_id(1)
    nb = _num_key_blocks(i, tq, tk, nkb, causal)
    if causal:
        q_t = i * tq + lax.broadcasted_iota(I32, (1, tq), 1)
        limit = (q_t // CHUNK + 1) * CHUNK
    else:
        limit = n_valid
    wit = wit_ref[0]

    def key_index(j):
        return j * tk + lax.broadcasted_iota(I32, (tk, tq), 0)

    def score_block(j, c):
        kb = ki_ref[0, j]
        s = jnp.zeros((tk, tq), F32)
        for hh in range(IDX_HEADS):
            lg = jnp.dot(kb, qit_ref[0, hh * IDX_DIM:(hh + 1) * IDX_DIM, :], preferred_element_type=F32)
            s = s + wit[hh:hh + 1, :] * jnp.maximum(lg, 0.0)
        bits = pltpu.bitcast(s, I32)
        bits = jnp.where(bits == INT_MIN, 0, bits)
        key = bits ^ ((bits >> 31) & 0x7FFFFFFF)
        keys_s[j] = jnp.where(key_index(j) < limit, key, INT_MIN)
        return c

    lax.fori_loop(0, nb, score_block, 0)

    def count(pred_fns):
        def body(j, accs):
            key = keys_s[j]
            return tuple(acc + jnp.sum(jnp.where(fn(key), 1.0, 0.0).reshape(tk // COUNT_ROWS, COUNT_ROWS, tq),
                                       axis=0)
                         for fn, acc in zip(pred_fns, accs))
        accs = lax.fori_loop(0, nb, body, tuple(jnp.zeros((COUNT_ROWS, tq), F32) for _ in pred_fns))
        return [jnp.sum(a, axis=0, keepdims=True) for a in accs]

    def bit_step(bi, carry):
        u, c_ge = carry
        cand = u | lax.shift_left(jnp.int32(1), 31 - bi)
        thr = cand ^ INT_MIN
        (cnt,) = count([lambda kk: kk >= thr])
        keep = cnt >= n_sel
        return jnp.where(keep, cand, u), jnp.where(keep, cnt, c_ge)

    n_adm = jnp.broadcast_to(jnp.asarray(limit, F32), (1, tq))
    u, c_ge = lax.fori_loop(0, 32, bit_step, (jnp.zeros((1, tq), I32), n_adm))
    thr = jnp.maximum(u ^ INT_MIN, INT_MIN + 1)
    tie_q = c_ge > n_sel
    has_tie = jnp.max(jnp.where(tie_q, 1.0, 0.0)) > 0.0

    @pl.when(jnp.logical_not(has_tie))
    def _():
        def bias_block(j, c):
            bias_ref[0, 0, j] = jnp.where(keys_s[j] >= thr, 0.0, MASK_NEG).astype(bias_ref.dtype)
            return c

        lax.fori_loop(0, nb, bias_block, 0)

    @pl.when(has_tie)
    def _():
        (c_gt,) = count([lambda kk: kk > thr])
        want = n_sel - c_gt
        tri = jnp.where(lax.broadcasted_iota(I32, (tk, tk), 0) >= lax.broadcasted_iota(I32, (tk, tk), 1),
                        1.0, 0.0).astype(jnp.bfloat16)

        def body(j, carry):
            before, below = carry
            eq = jnp.where(keys_s[j] == thr, 1.0, 0.0).astype(jnp.bfloat16)
            pref = before + jnp.dot(tri, eq, preferred_element_type=F32)
            below = below + jnp.sum(jnp.where(pref < want, 1.0, 0.0), axis=0, keepdims=True)
            return pref[tk - 1:tk, :], below

        _, below = lax.fori_loop(0, nb, body, (jnp.zeros((1, tq), F32), jnp.zeros((1, tq), F32)))
        jstar = jnp.where(tie_q, below.astype(I32), NO_TIE_LIMIT)

        def bias_block(j, c):
            key = keys_s[j]
            tie_ok = jnp.where(key_index(j) <= jstar, 0.0, MASK_NEG)
            bias = jnp.where(key > thr, 0.0, jnp.where(key == thr, tie_ok, MASK_NEG))
            bias_ref[0, 0, j] = bias.astype(bias_ref.dtype)
            return c

        lax.fori_loop(0, nb, bias_block, 0)

    def masked_block(j, c):
        bias_ref[0, 0, j] = jnp.full((tk, tq), MASK_NEG, bias_ref.dtype)
        return c

    lax.fori_loop(nb, nkb, masked_block, 0)


def _select(qit, wit, ki_blk, tq, causal, n_valid, n_sel):
    b, width, t = qit.shape
    _, nkb, tk, _ = ki_blk.shape
    nq = t // tq
    return pl.pallas_call(
        functools.partial(_select_kernel, tq=tq, tk=tk, nkb=nkb, causal=causal, n_valid=n_valid, n_sel=n_sel),
        grid=(b, nq),
        in_specs=[pl.BlockSpec((1, width, tq), lambda i, j: (i, 0, j)),
                  pl.BlockSpec((1, IDX_HEADS, tq), lambda i, j: (i, 0, j)),
                  pl.BlockSpec((1, nkb, tk, IDX_DIM), lambda i, j: (i, 0, 0, 0))],
        out_specs=pl.BlockSpec((1, 1, nkb, tk, tq), lambda i, j: (i, j, 0, 0, 0)),
        out_shape=jax.ShapeDtypeStruct((b, nq, nkb, tk, tq), jnp.bfloat16),
        scratch_shapes=[pltpu.VMEM((nkb, tk, tq), I32)],
        compiler_params=_cparams(("arbitrary", "arbitrary")),
        name="topk_select",
    )(qit, wit, ki_blk)


def _attn_kernel(qtile_ref, kblock_ref, qt_ref, k_ref, vt_ref, *rest, tq, tk, nkb, causal, n_bias):
    bias_refs = rest[:n_bias]
    o_ref, m_s, l_s, acc_s, s_s, bias_s = rest[n_bias:]
    step = pl.program_id(1)
    j = kblock_ref[step]
    last = _num_key_blocks(qtile_ref[step], tq, tk, nkb, causal) - 1

    @pl.when(j == 0)
    def _():
        m_s[...] = jnp.full(m_s.shape, M_INIT, F32)
        l_s[...] = jnp.zeros(l_s.shape, F32)
        acc_s[...] = jnp.zeros(acc_s.shape, F32)

    sub = _attn_chunk(tq, tk)
    chunks = [slice(c * sub, (c + 1) * sub) for c in range(tk // sub)]
    m_all = m_s[...]
    l_all = l_s[...]
    tqs = tq // n_bias
    for n, bias_ref in enumerate(bias_refs):
        bias_s[:, n * tqs:(n + 1) * tqs] = bias_ref[0, 0, 0].astype(F32)

    def stage_a(hh, keys, mx):
        s = jnp.dot(k_ref[0, hh, keys, :], qt_ref[0, hh * HEAD_DIM:(hh + 1) * HEAD_DIM, :],
                    preferred_element_type=F32) + bias_s[keys, :]
        s_s[hh % 2, keys, :] = s
        return jnp.maximum(mx, jnp.max(s, axis=0, keepdims=True))

    def stage_b(hh, keys, m_new, lsum, pv):
        p = jnp.exp2(s_s[hh % 2, keys, :] - m_new)
        lsum = lsum + jnp.sum(p, axis=0, keepdims=True)
        pv = pv + jnp.dot(vt_ref[0, hh * HEAD_DIM:(hh + 1) * HEAD_DIM, keys], p.astype(MXU_DTYPE),
                          preferred_element_type=F32)
        return lsum, pv

    m_new = m_all[0:1, :]
    for keys in chunks:
        m_new = stage_a(0, keys, m_new)
    m_out, l_out = [], []
    for hh in range(N_HEADS):
        rows = slice(hh * HEAD_DIM, (hh + 1) * HEAD_DIM)
        lsum = jnp.zeros((1, tq), F32)
        pv = jnp.zeros((HEAD_DIM, tq), F32)
        m_next = m_all[hh + 1:hh + 2, :] if hh + 1 < N_HEADS else None
        for keys in chunks:
            if m_next is not None:
                m_next = stage_a(hh + 1, keys, m_next)
            lsum, pv = stage_b(hh, keys, m_new, lsum, pv)
        alpha = jnp.exp2(m_all[hh:hh + 1, :] - m_new)
        l_out.append(alpha * l_all[hh:hh + 1, :] + lsum)
        acc_s[rows, :] = alpha * acc_s[rows, :] + pv
        m_out.append(m_new)
        m_new = m_next
    m_s[...] = jnp.concatenate(m_out, axis=0)
    l_s[...] = jnp.concatenate(l_out, axis=0)

    @pl.when(j == last)
    def _():
        out_t = jnp.concatenate(
            [acc_s[hh * HEAD_DIM:(hh + 1) * HEAD_DIM, :] / l_s[hh:hh + 1, :] for hh in range(N_HEADS)], axis=0)
        o_ref[0] = out_t.T


def _attn_chunk(tq, tk):
    lane_tiles = -(-tq // LANES)
    for sub in (256, 128, 64, 32, 16, 8):
        if tk % sub == 0 and (sub // SUBLANES) * lane_tiles <= 32:
            return sub
    raise AssertionError((tq, tk))


def _attention(qt, k_hm, vt, bias, tq, tk, causal):
    b, width, t = qt.shape
    s_len = vt.shape[-1]
    tqs = bias.shape[-1]
    n_bias = tq // tqs
    nq, nkb = t // tq, s_len // tk
    pairs = [(i, j) for i in range(nq) for j in range(_num_key_blocks(i, tq, tk, nkb, causal))]
    qtile = jnp.asarray([p[0] for p in pairs], I32)
    kblock = jnp.asarray([p[1] for p in pairs], I32)

    def bias_spec(n):
        return pl.BlockSpec((1, 1, 1, tk, tqs), lambda bb, s, qi, kj: (bb, qi[s] * n_bias + n, kj[s], 0, 0))

    grid_spec = pltpu.PrefetchScalarGridSpec(
        num_scalar_prefetch=2,
        grid=(b, len(pairs)),
        in_specs=[pl.BlockSpec((1, width, tq), lambda bb, s, qi, kj: (bb, 0, qi[s])),
                  pl.BlockSpec((1, N_HEADS, tk, HEAD_DIM), lambda bb, s, qi, kj: (bb, 0, kj[s], 0)),
                  pl.BlockSpec((1, width, tk), lambda bb, s, qi, kj: (bb, 0, kj[s]))]
                 + [bias_spec(n) for n in range(n_bias)],
        out_specs=pl.BlockSpec((1, tq, width), lambda bb, s, qi, kj: (bb, qi[s], 0)),
        scratch_shapes=[pltpu.VMEM((N_HEADS, tq), F32), pltpu.VMEM((N_HEADS, tq), F32),
                        pltpu.VMEM((width, tq), F32), pltpu.VMEM((2, tk, tq), F32),
                        pltpu.VMEM((tk, tq), F32)])
    return pl.pallas_call(
        functools.partial(_attn_kernel, tq=tq, tk=tk, nkb=nkb, causal=causal, n_bias=n_bias),
        grid_spec=grid_spec,
        out_shape=jax.ShapeDtypeStruct((b, t, width), F32),
        compiler_params=_cparams(("arbitrary", "arbitrary")),
        name="masked_attention",
    )(qtile, kblock, qt, k_hm, vt, *([bias] * n_bias))


def _merge_kernel(x_ref, ys_ref, att_ref, za_ref, gs_ref, ga_ref, gate_ref, wps_ref, wpa_ref, wo_ref,
                  gfin_ref, o_ref, *, final_norm):
    ya = att_ref[0] * _silu(za_ref[0])
    merged = (jax.nn.sigmoid(gs_ref[0]) * _mxu_dot(ys_ref[...], wps_ref[...])
              + jax.nn.sigmoid(ga_ref[0]) * _mxu_dot(ya, wpa_ref[...]))
    x_new = x_ref[0] + gate_ref[0] * _mxu_dot(merged, wo_ref[...])
    if final_norm:
        x_new = x_new * lax.rsqrt(jnp.mean(x_new * x_new, axis=-1, keepdims=True) + EPS) * gfin_ref[...]
    o_ref[0] = x_new


def _merge(x, ys, att, za, gs, ga, gate, w_ps, w_pa, w_o, g_final, tm, final_norm):
    b, t, d = x.shape
    half = d // 2
    row = lambda width: pl.BlockSpec((1, tm, width), lambda i, j: (i, j, 0))
    const = lambda shape: pl.BlockSpec(shape, lambda i, j: (0,) * len(shape))
    return pl.pallas_call(
        functools.partial(_merge_kernel, final_norm=final_norm),
        grid=(b, t // tm),
        in_specs=[row(d), pl.BlockSpec((tm, half), lambda i, j: (j, i)), row(half), row(half), row(d), row(d),
                  pl.BlockSpec((1, 1, d), lambda i, j: (i, 0, 0)),
                  const((half, d)), const((half, d)), const((d, d)), const((1, d))],
        out_specs=row(d),
        out_shape=jax.ShapeDtypeStruct((b, t, d), F32),
        compiler_params=_cparams(("arbitrary", "arbitrary")),
        name="merge_out",
    )(x, ys, att, za, gs, ga, gate, w_ps.astype(MXU_DTYPE), w_pa.astype(MXU_DTYPE),
      w_o.astype(MXU_DTYPE), g_final.reshape(1, d))


def _row_tile(t, cap):
    tile = min(t, cap)
    assert t % tile == 0, (t, tile)
    return tile


def _key_blocks(s_len):
    s_pad = -(-s_len // LANES) * LANES
    for tk in (512, 384, 256, 128):
        if s_pad % tk == 0:
            return s_pad, tk
    raise AssertionError(s_len)


def _pad_keys(a, s_pad, axis):
    pad = s_pad - a.shape[axis]
    if pad == 0:
        return a
    widths = [(0, 0)] * a.ndim
    widths[axis] = (0, pad)
    return jnp.pad(a, widths)


def _layer(x, mod, lw, ssm_par, h0_re, h0_im, past, final_norm):
    b, t, d = x.shape
    shift, scale, gate = (m.reshape(b, 1, d) for m in jnp.split(mod, 3, axis=-1))
    tm = _row_tile(t, 256)
    (u, zs, za, gs, ga, k, v, ki, wit, qt, qit, vt, k_hm) = _inproj(
        x, shift, scale, lw["g_norm"], lw["wmain"], lw["wsmall"], tm)

    wb, wc, coef = ssm_par
    half = d // 2
    ys, hr, hi = _ssm(u.reshape(t * b, half), zs.reshape(t * b, half), b, h0_re, h0_im, wb, wc, coef,
                      lw["d_skip"], lw["w_glu"], lw["b_glu"], min(t * b, 256))
    ys = ys.reshape(t, b * half)

    ki_bf = ki.astype(MXU_DTYPE)
    if past is None:
        causal, n_valid = True, t
    else:
        past_k, past_v, past_ki = past
        causal, n_valid = False, past_k.shape[1] + t
        k_hm = jnp.concatenate([past_k.transpose(0, 2, 1, 3).astype(MXU_DTYPE), k_hm], axis=2)
        vt = jnp.concatenate([past_v.reshape(b, -1, d // 2).transpose(0, 2, 1).astype(MXU_DTYPE), vt], axis=2)
        ki_bf = jnp.concatenate([past_ki.astype(MXU_DTYPE), ki_bf], axis=1)
    n_sel = min(TOPK_MAX, n_valid // 4)
    s_pad, tk = _key_blocks(n_valid)
    k_hm = _pad_keys(k_hm, s_pad, 2)
    vt = _pad_keys(vt, s_pad, 2)
    ki_blk = _pad_keys(ki_bf, s_pad, 1).reshape(b, s_pad // tk, tk, IDX_DIM)

    tq = _row_tile(t, 256)
    bias = _select(qit, wit, ki_blk, tq, causal, n_valid, n_sel)
    att = _attention(qt, k_hm, vt, bias, tq, tk, causal)

    x_new = _merge(x, ys, att, za, gs, ga, gate, lw["w_ps"], lw["w_pa"], lw["w_o"], lw["g_final"],
                   _row_tile(t, 256), final_norm)
    gshape = (b, -1, STATE)
    return x_new, (k.reshape(b, t, N_HEADS, HEAD_DIM), v.reshape(b, t, N_HEADS, HEAD_DIM), ki,
                   hr.reshape(gshape), hi.reshape(gshape))


def kernel(x_prompt, x_sample, cache_k, cache_v, cache_kidx, state_ssm_re, state_ssm_im, c_prompt, c_sample, w_mod, b_mod, g_norm, w_in, a_re, a_im, log_dt, b_re, b_im, c_re, c_im, d_skip, w_glu, b_glu, w_ps, w_pa, w_o, g_final):
    depth = w_mod.shape[0]
    bp, _, d = x_prompt.shape
    g, p = a_re.shape[1:]
    mod = _modulation(jnp.concatenate([c_prompt, c_sample], axis=0), w_mod, b_mod)
    zeros = jnp.zeros((bp, g * p), F32)
    xp, xs = x_prompt, x_sample
    outs_p, outs_s = [], []
    for l in range(depth):
        wmain, wsmall = _inproj_weights(w_in[l], d)
        lw = dict(g_norm=g_norm[l], wmain=wmain, wsmall=wsmall, d_skip=d_skip[l], w_glu=w_glu[l],
                  b_glu=b_glu[l], w_ps=w_ps[l], w_pa=w_pa[l], w_o=w_o[l], g_final=g_final)
        ssm_par = _ssm_prep(a_re[l], a_im[l], log_dt[l], b_re[l], b_im[l], c_re[l], c_im[l])
        last = l == depth - 1
        xp, op = _layer(xp, mod[l, :bp], lw, ssm_par, zeros, zeros, None, last)
        xs, os_ = _layer(xs, mod[l, bp:], lw, ssm_par, state_ssm_re[l].reshape(-1, g * p),
                         state_ssm_im[l].reshape(-1, g * p), (cache_k[l], cache_v[l], cache_kidx[l]), last)
        outs_p.append(op)
        outs_s.append(os_)
    stack = lambda outs: tuple(jnp.stack(z) for z in zip(*outs))
    return (xp, xs) + stack(outs_p) + stack(outs_s)
```

```python
import functools
import math

import jax
import jax.numpy as jnp
from jax import lax
from jax.experimental import pallas as pl
from jax.experimental.pallas import tpu as pltpu

F32 = jnp.float32
I32 = jnp.int32
MXU_DTYPE = jnp.bfloat16

CHUNK = 64
SSM_GROUP = 16
STATE = 64
N_HEADS = 8
HEAD_DIM = 64
IDX_HEADS = 8
IDX_DIM = 64
TOPK_MAX = 256
EPS = 1e-6
LOG2E = math.log2(math.e)

LANES = 128
SUBLANES = 8
V7X_VMEM_BYTES = 64 * 1024 * 1024
VMEM_LIMIT = V7X_VMEM_BYTES - 8 * 1024 * 1024

INT_MIN = -(2 ** 31)
INT16_MIN = -(2 ** 15)
MASK_NEG = -1e30
M_INIT = -1e29
NO_TIE_LIMIT = 2 ** 30
COUNT_ROWS = 32


def _cparams(sem):
    return pltpu.CompilerParams(dimension_semantics=sem, vmem_limit_bytes=VMEM_LIMIT)


def _silu(x):
    return x * jax.nn.sigmoid(x)


def _mxu_dot(a, b):
    return jnp.dot(a.astype(MXU_DTYPE), b.astype(MXU_DTYPE), preferred_element_type=F32)


def _split_hi_lo(a):
    hi = a.astype(jnp.bfloat16)
    lo = (a - hi.astype(F32)).astype(jnp.bfloat16)
    return hi, lo


def _dot_3pass(a, b):
    ah, al = _split_hi_lo(a)
    bh, bl = _split_hi_lo(b)
    d = functools.partial(jnp.dot, preferred_element_type=F32)
    return d(ah, bh) + (d(ah, bl) + d(al, bh))


def _mod_kernel(c_ref, w_ref, b_ref, o_ref):
    o_ref[...] = _dot_3pass(_silu(c_ref[...]), w_ref[...]) + b_ref[...]


def _modulation(c_all, w_mod, b_mod):
    depth, d, d3 = w_mod.shape
    rows = c_all.shape[0]
    cols = d3 // 4
    return pl.pallas_call(
        _mod_kernel,
        grid=(depth, d3 // cols),
        in_specs=[
            pl.BlockSpec((rows, d), lambda l, j: (0, 0)),
            pl.BlockSpec((None, d, cols), lambda l, j: (l, 0, j)),
            pl.BlockSpec((None, 1, cols), lambda l, j: (l, 0, j)),
        ],
        out_specs=pl.BlockSpec((None, rows, cols), lambda l, j: (l, 0, j)),
        out_shape=jax.ShapeDtypeStruct((depth, rows, d3), F32),
        compiler_params=_cparams(("arbitrary", "arbitrary")),
        name="adaln_mod",
    )(c_all, w_mod, b_mod.reshape(depth, 1, d3))


def _ssm_prep_kernel(are_ref, aim_ref, ldt_ref, bre_ref, bim_ref, cre_ref, cim_ref,
                     wb_ref, wc_ref, coef_ref):
    a_re = are_ref[...]
    a_im = aim_ref[...]
    dt = jnp.exp(ldt_ref[...])
    gp = a_re.shape[-1]

    mag = jnp.exp(a_re * dt)
    ang = a_im * dt
    ab_re, ab_im = mag * jnp.cos(ang), mag * jnp.sin(ang)
    den = a_re * a_re + a_im * a_im
    nr = ab_re - 1.0
    ni = ab_im
    f_re = (nr * a_re + ni * a_im) / den
    f_im = (ni * a_re - nr * a_im) / den

    rows_b = lax.broadcasted_iota(I32, bre_ref.shape, 0) // SSM_GROUP
    cols_b = lax.broadcasted_iota(I32, bre_ref.shape, 1) // STATE
    on_b = rows_b == cols_b
    b_re = bre_ref[...]
    b_im = bim_ref[...]
    wb_ref[:, 0:gp] = jnp.where(on_b, f_re * b_re - f_im * b_im, 0.0).astype(wb_ref.dtype)
    wb_ref[:, gp:2 * gp] = jnp.where(on_b, f_re * b_im + f_im * b_re, 0.0).astype(wb_ref.dtype)

    rows_c = lax.broadcasted_iota(I32, cre_ref.shape, 0) // STATE
    cols_c = lax.broadcasted_iota(I32, cre_ref.shape, 1) // SSM_GROUP
    on_c = rows_c == cols_c
    wc_ref[0:gp, :] = jnp.where(on_c, cre_ref[...], 0.0).astype(wc_ref.dtype)
    wc_ref[gp:2 * gp, :] = jnp.where(on_c, -cim_ref[...], 0.0).astype(wc_ref.dtype)

    coef_ref[0] = jnp.broadcast_to(ab_re, (SUBLANES, gp))
    coef_ref[1] = jnp.broadcast_to(ab_im, (SUBLANES, gp))


def _ssm_prep(a_re, a_im, log_dt, b_re, b_im, c_re, c_im):
    g, p = a_re.shape
    n = b_re.shape[-1]
    gp, gn = g * p, g * n
    row = lambda a: a.reshape(1, gp)
    ldt = jnp.broadcast_to(log_dt[:, None], (g, p)).reshape(1, gp)
    b_exp = lambda b: jnp.broadcast_to(b.transpose(2, 0, 1).reshape(1, n, gp), (g, n, gp)).reshape(gn, gp)
    c_exp = lambda c: jnp.broadcast_to(c.transpose(0, 2, 1).reshape(gp, 1, n), (gp, g, n)).reshape(gp, gn)
    return pl.pallas_call(
        _ssm_prep_kernel,
        out_shape=(jax.ShapeDtypeStruct((gn, 2 * gp), MXU_DTYPE),
                   jax.ShapeDtypeStruct((2 * gp, gn), MXU_DTYPE),
                   jax.ShapeDtypeStruct((2, SUBLANES, gp), F32)),
        compiler_params=pltpu.CompilerParams(vmem_limit_bytes=VMEM_LIMIT),
        name="ssm_prep",
    )(row(a_re), row(a_im), ldt, b_exp(b_re), b_exp(b_im), c_exp(c_re), c_exp(c_im))


_MAIN_SEGS = ("u", "zs", "za", "gs", "ga", "k", "v", "q", "qi")


def _main_offsets(d):
    half = d // 2
    widths = dict(u=half, zs=half, za=half, gs=d, ga=d, k=half, v=half, q=half, qi=half)
    offs, o = {}, 0
    for name in _MAIN_SEGS:
        offs[name] = (o, o + widths[name])
        o += widths[name]
    return offs, o


def _inproj_kernel(x_ref, shift_ref, scale_ref, g_ref, wmain_ref, wsmall_ref,
                   u_ref, zs_ref, za_ref, gs_ref, ga_ref, k_ref, v_ref, ki_ref,
                   wit_ref, qt_ref, qit_ref, vt_ref, khm_ref, *, offs):
    x = x_ref[0]
    h = x * lax.rsqrt(jnp.mean(x * x, axis=-1, keepdims=True) + EPS) * g_ref[...]
    h = h * (1.0 + scale_ref[0]) + shift_ref[0]
    hb = h.astype(MXU_DTYPE)

    def proj(name):
        c0, c1 = offs[name]
        return jnp.dot(hb, wmain_ref[:, c0:c1], preferred_element_type=F32)

    u_ref[...] = proj("u")
    zs_ref[...] = proj("zs")
    za_ref[0] = proj("za")
    gs_ref[0] = proj("gs")
    ga_ref[0] = proj("ga")

    zk = proj("k")
    k_ref[0] = zk
    for hh in range(N_HEADS):
        khm_ref[0, hh] = zk[:, hh * HEAD_DIM:(hh + 1) * HEAD_DIM].astype(khm_ref.dtype)
    zv = proj("v")
    v_ref[0] = zv
    vt_ref[0] = zv.T.astype(vt_ref.dtype)
    qt_ref[0] = proj("q").T.astype(qt_ref.dtype)
    qit_ref[0] = proj("qi").T.astype(qit_ref.dtype)

    zsm = jnp.dot(hb, wsmall_ref[...], preferred_element_type=F32)
    ki_ref[0] = zsm[:, 0:IDX_DIM]
    wit_ref[0] = zsm.T[IDX_DIM:IDX_DIM + IDX_HEADS, :] * (IDX_HEADS ** -0.5)


def _inproj_weights(w_in, d):
    half = d // 2
    idx_w = IDX_HEADS * IDX_DIM
    o = 0
    seg = {}
    for name, width in (("u", half), ("zs", half), ("q", half), ("k", half), ("v", half), ("za", half),
                        ("qi", idx_w), ("ki", IDX_DIM), ("wi", IDX_HEADS), ("gs", d), ("ga", d)):
        seg[name] = w_in[:, o:o + width]
        o += width
    seg["q"] = seg["q"] * (HEAD_DIM ** -0.5 * LOG2E)
    seg["qi"] = seg["qi"] * (IDX_DIM ** -0.5)
    wmain = jnp.concatenate([seg[n] for n in _MAIN_SEGS], axis=1).astype(MXU_DTYPE)
    wsmall = jnp.concatenate(
        [seg["ki"], seg["wi"], jnp.zeros((d, LANES - IDX_DIM - IDX_HEADS), w_in.dtype)], axis=1).astype(MXU_DTYPE)
    return wmain, wsmall


def _inproj(x, shift, scale, g_norm, wmain, wsmall, tm):
    b, t, d = x.shape
    half = d // 2
    offs, wcols = _main_offsets(d)
    nt = t // tm
    row = lambda width: pl.BlockSpec((1, tm, width), lambda i, j: (i, j, 0))
    hm = pl.BlockSpec((1, N_HEADS, tm, HEAD_DIM), lambda i, j: (i, 0, j, 0))
    tr = lambda rows: pl.BlockSpec((1, rows, tm), lambda i, j: (i, 0, j))
    tb = pl.BlockSpec((tm, half), lambda i, j: (j, i))
    per_b = pl.BlockSpec((1, 1, d), lambda i, j: (i, 0, 0))
    const = lambda shape: pl.BlockSpec(shape, lambda i, j: (0,) * len(shape))
    f = lambda *s: jax.ShapeDtypeStruct(s, F32)
    bf = lambda *s: jax.ShapeDtypeStruct(s, MXU_DTYPE)
    return pl.pallas_call(
        functools.partial(_inproj_kernel, offs=offs),
        grid=(b, nt),
        in_specs=[row(d), per_b, per_b, const((1, d)), const((d, wcols)), const((d, LANES))],
        out_specs=[tb, tb, row(half), row(d), row(d), row(half), row(half),
                   row(IDX_DIM), tr(IDX_HEADS), tr(half), tr(half), tr(half), hm],
        out_shape=[f(t, b * half), f(t, b * half), f(b, t, half), f(b, t, d), f(b, t, d),
                   f(b, t, half), f(b, t, half), f(b, t, IDX_DIM), f(b, IDX_HEADS, t),
                   bf(b, half, t), bf(b, half, t), bf(b, half, t), bf(b, N_HEADS, t, HEAD_DIM)],
        compiler_params=_cparams(("arbitrary", "arbitrary")),
        name="inproj",
    )(x, shift, scale, g_norm.reshape(1, d), wmain, wsmall)


def _ssm_kernel(u_ref, zs_ref, h0r_ref, h0i_ref, wb_ref, wc_ref, coef_ref, d_ref, wglu_ref, bglu_ref,
                y_ref, hr_ref, hi_ref, bu_s, *, tt, nb, gp):
    @pl.when(pl.program_id(0) == 0)
    def _():
        hr_ref[...] = h0r_ref[...]
        hi_ref[...] = h0i_ref[...]

    u = u_ref[...]
    ub = u.astype(MXU_DTYPE)
    dn = u.shape[-1]
    slabs = dn // LANES
    sw = gp // slabs
    for j in range(slabs):
        ch = slice(j * LANES, (j + 1) * LANES)
        for part in range(2):
            st = slice(part * gp + j * sw, part * gp + (j + 1) * sw)
            bu_s[:, st] = jnp.dot(ub[:, ch], wb_ref[ch, st], preferred_element_type=F32)

    groups = nb // SUBLANES

    def step(t, carry):
        a_re = coef_ref[0]
        a_im = coef_ref[1]
        out = []
        for g in range(groups):
            h_re, h_im = carry[2 * g], carry[2 * g + 1]
            rows = pl.ds(pl.multiple_of(t * nb + g * SUBLANES, SUBLANES), SUBLANES)
            n_re = (a_re * h_re - a_im * h_im) + bu_s[rows, 0:gp]
            n_im = (a_re * h_im + a_im * h_re) + bu_s[rows, gp:2 * gp]
            bu_s[rows, 0:gp] = n_re
            bu_s[rows, gp:2 * gp] = n_im
            out += [n_re, n_im]
        return tuple(out)

    init = []
    for g in range(groups):
        rows = slice(g * SUBLANES, (g + 1) * SUBLANES)
        init += [hr_ref[rows, :], hi_ref[rows, :]]
    fin = lax.fori_loop(0, tt, step, tuple(init), unroll=2)
    for g in range(groups):
        rows = slice(g * SUBLANES, (g + 1) * SUBLANES)
        hr_ref[rows, :] = fin[2 * g]
        hi_ref[rows, :] = fin[2 * g + 1]

    ys = []
    for j in range(slabs):
        ch = slice(j * LANES, (j + 1) * LANES)
        acc = d_ref[:, ch] * u[:, ch]
        for part in range(2):
            st = slice(part * gp + j * sw, part * gp + (j + 1) * sw)
            acc = acc + jnp.dot(bu_s[:, st].astype(MXU_DTYPE), wc_ref[st, ch], preferred_element_type=F32)
        ys.append(acc)
    y = jax.nn.gelu(jnp.concatenate(ys, axis=-1))
    y = y * jax.nn.sigmoid(_mxu_dot(y, wglu_ref[...]) + bglu_ref[...])
    y_ref[...] = y * _silu(zs_ref[...])


def _ssm(u_tb, zs_tb, nb, h0_re, h0_im, wb, wc, coef, d_skip, w_glu, b_glu, rows_per_step):
    rows, dn = u_tb.shape
    gp = wb.shape[1] // 2
    assert nb % SUBLANES == 0 and rows_per_step % nb == 0 and rows % rows_per_step == 0
    tt = rows_per_step // nb
    row = pl.BlockSpec((rows_per_step, dn), lambda j: (j, 0))
    const = lambda shape: pl.BlockSpec(shape, lambda j: (0,) * len(shape))
    return pl.pallas_call(
        functools.partial(_ssm_kernel, tt=tt, nb=nb, gp=gp),
        grid=(rows // rows_per_step,),
        in_specs=[row, row, const((nb, gp)), const((nb, gp)), const(wb.shape), const(wc.shape),
                  const(coef.shape), const((1, dn)), const(w_glu.shape), const((1, dn))],
        out_specs=[row, const((nb, gp)), const((nb, gp))],
        out_shape=[jax.ShapeDtypeStruct((rows, dn), F32),
                   jax.ShapeDtypeStruct((nb, gp), F32),
                   jax.ShapeDtypeStruct((nb, gp), F32)],
        scratch_shapes=[pltpu.VMEM((rows_per_step, 2 * gp), F32)],
        compiler_params=_cparams(("arbitrary",)),
        name="ssm_scan",
    )(u_tb, zs_tb, h0_re, h0_im, wb, wc, coef,
      d_skip.reshape(1, dn), w_glu.astype(MXU_DTYPE), b_glu.reshape(1, dn))


def _num_key_blocks(i, tq, tk, nkb, causal):
    if not causal:
        return nkb
    return ((i + 1) * tq + (tk - 1)) // tk


def _select_kernel(qit_ref, wit_ref, ki_ref, bias_ref, keys_s, hi_s, lo_s, *,
                   tq, tk, nkb, causal, n_valid, n_sel):
    i = pl.program_id(1)
    nb = _num_key_blocks(i, tq, tk, nkb, causal)
    if causal:
        q_t = i * tq + lax.broadcasted_iota(I32, (1, tq), 1)
        limit = (q_t // CHUNK + 1) * CHUNK
    else:
        limit = n_valid
    wit = wit_ref[0]

    def key_index(j):
        return j * tk + lax.broadcasted_iota(I32, (tk, tq), 0)

    def score_block(j, c):
        kb = ki_ref[0, j]
        s = jnp.zeros((tk, tq), F32)
        for hh in range(IDX_HEADS):
            lg = jnp.dot(kb, qit_ref[0, hh * IDX_DIM:(hh + 1) * IDX_DIM, :], preferred_element_type=F32)
            s = s + wit[hh:hh + 1, :] * jnp.maximum(lg, 0.0)
        bits = pltpu.bitcast(s, I32)
        bits = jnp.where(bits == INT_MIN, 0, bits)
        key = bits ^ ((bits >> 31) & 0x7FFFFFFF)
        key = jnp.where(key_index(j) < limit, key, INT_MIN)
        keys_s[j] = key
        hi_s[j] = (key >> 16).astype(jnp.int16)
        return c

    lax.fori_loop(0, nb, score_block, 0)

    def count(pred_fns):
        def body(j, accs):
            key = keys_s[j]
            return tuple(acc + jnp.sum(jnp.where(fn(key), 1.0, 0.0).reshape(tk // COUNT_ROWS, COUNT_ROWS, tq),
                                       axis=0)
                         for fn, acc in zip(pred_fns, accs))
        accs = lax.fori_loop(0, nb, body, tuple(jnp.zeros((COUNT_ROWS, tq), F32) for _ in pred_fns))
        return [jnp.sum(a, axis=0, keepdims=True) for a in accs]

    def count16(ref, thr16, strict):
        def body(j, acc):
            blk = ref[j]
            hit = blk > thr16 if strict else blk >= thr16
            ind = jnp.where(hit, jnp.int16(1), jnp.int16(0))
            part = ind[0:COUNT_ROWS]
            for r in range(1, tk // COUNT_ROWS):
                part = part + ind[r * COUNT_ROWS:(r + 1) * COUNT_ROWS]
            return acc + part.astype(F32)
        acc = lax.fori_loop(0, nb, body, jnp.zeros((COUNT_ROWS, tq), F32))
        return jnp.sum(acc, axis=0, keepdims=True)

    def radix16(ref, need, count_all):
        def bit_step(bi, carry):
            u, c_acc = carry
            cand = u | lax.shift_left(jnp.int32(1), 15 - bi)
            cnt = count16(ref, (cand + INT16_MIN).astype(jnp.int16), False)
            keep = cnt >= need
            return jnp.where(keep, cand, u), jnp.where(keep, cnt, c_acc)
        u, c_acc = lax.fori_loop(0, 16, bit_step, (jnp.zeros((1, tq), I32), count_all))
        return u + INT16_MIN, c_acc

    n_adm = jnp.broadcast_to(jnp.asarray(limit, F32), (1, tq))
    n_all = jnp.broadcast_to((nb * tk).astype(F32) if causal else jnp.float32(nb * tk), (1, tq))
    hi_sel, c_ge_hi = radix16(hi_s, n_sel, n_all)
    c_gt_hi = count16(hi_s, hi_sel.astype(jnp.int16), True)
    want_lo = n_sel - c_gt_hi

    def low_block(j, c):
        key = keys_s[j]
        low = (key & 0xFFFF) + INT16_MIN
        lo_s[j] = jnp.where((key >> 16) == hi_sel, low, INT16_MIN).astype(jnp.int16)
        return c

    lax.fori_loop(0, nb, low_block, 0)
    lo_sel, c_ge_lo = radix16(lo_s, want_lo, c_ge_hi - c_gt_hi)
    kth = lax.shift_left(hi_sel, 16) | (lo_sel - INT16_MIN)
    c_ge = jnp.where(kth == INT_MIN, n_adm, c_gt_hi + c_ge_lo)
    thr = jnp.maximum(kth, INT_MIN + 1)
    tie_q = c_ge > n_sel
    has_tie = jnp.max(jnp.where(tie_q, 1.0, 0.0)) > 0.0

    @pl.when(jnp.logical_not(has_tie))
    def _():
        def bias_block(j, c):
            bias_ref[0, 0, j] = jnp.where(keys_s[j] >= thr, 0.0, MASK_NEG).astype(bias_ref.dtype)
            return c

        lax.fori_loop(0, nb, bias_block, 0)

    @pl.when(has_tie)
    def _():
        (c_gt,) = count([lambda kk: kk > thr])
        want = n_sel - c_gt
        tri = jnp.where(lax.broadcasted_iota(I32, (tk, tk), 0) >= lax.broadcasted_iota(I32, (tk, tk), 1),
                        1.0, 0.0).astype(jnp.bfloat16)

        def body(j, carry):
            before, below = carry
            eq = jnp.where(keys_s[j] == thr, 1.0, 0.0).astype(jnp.bfloat16)
            pref = before + jnp.dot(tri, eq, preferred_element_type=F32)
            below = below + jnp.sum(jnp.where(pref < want, 1.0, 0.0), axis=0, keepdims=True)
            return pref[tk - 1:tk, :], below

        _, below = lax.fori_loop(0, nb, body, (jnp.zeros((1, tq), F32), jnp.zeros((1, tq), F32)))
        jstar = jnp.where(tie_q, below.astype(I32), NO_TIE_LIMIT)

        def bias_block(j, c):
            key = keys_s[j]
            tie_ok = jnp.where(key_index(j) <= jstar, 0.0, MASK_NEG)
            bias = jnp.where(key > thr, 0.0, jnp.where(key == thr, tie_ok, MASK_NEG))
            bias_ref[0, 0, j] = bias.astype(bias_ref.dtype)
            return c

        lax.fori_loop(0, nb, bias_block, 0)

    def masked_block(j, c):
        bias_ref[0, 0, j] = jnp.full((tk, tq), MASK_NEG, bias_ref.dtype)
        return c

    lax.fori_loop(nb, nkb, masked_block, 0)


def _select(qit, wit, ki_blk, tq, causal, n_valid, n_sel):
    b, width, t = qit.shape
    _, nkb, tk, _ = ki_blk.shape
    nq = t // tq
    return pl.pallas_call(
        functools.partial(_select_kernel, tq=tq, tk=tk, nkb=nkb, causal=causal, n_valid=n_valid, n_sel=n_sel),
        grid=(b, nq),
        in_specs=[pl.BlockSpec((1, width, tq), lambda i, j: (i, 0, j)),
                  pl.BlockSpec((1, IDX_HEADS, tq), lambda i, j: (i, 0, j)),
                  pl.BlockSpec((1, nkb, tk, IDX_DIM), lambda i, j: (i, 0, 0, 0))],
        out_specs=pl.BlockSpec((1, 1, nkb, tk, tq), lambda i, j: (i, j, 0, 0, 0)),
        out_shape=jax.ShapeDtypeStruct((b, nq, nkb, tk, tq), jnp.bfloat16),
        scratch_shapes=[pltpu.VMEM((nkb, tk, tq), I32), pltpu.VMEM((nkb, tk, tq), jnp.int16),
                        pltpu.VMEM((nkb, tk, tq), jnp.int16)],
        compiler_params=_cparams(("arbitrary", "arbitrary")),
        name="topk_select",
    )(qit, wit, ki_blk)


def _attn_kernel(qtile_ref, kblock_ref, qt_ref, k_ref, vt_ref, *rest, tq, tk, nkb, causal, n_bias):
    bias_refs = rest[:n_bias]
    o_ref, m_s, l_s, acc_s, s_s, bias_s = rest[n_bias:]
    step = pl.program_id(1)
    j = kblock_ref[step]
    last = _num_key_blocks(qtile_ref[step], tq, tk, nkb, causal) - 1

    @pl.when(j == 0)
    def _():
        m_s[...] = jnp.full(m_s.shape, M_INIT, F32)
        l_s[...] = jnp.zeros(l_s.shape, F32)
        acc_s[...] = jnp.zeros(acc_s.shape, F32)

    sub = _attn_chunk(tq, tk)
    chunks = [slice(c * sub, (c + 1) * sub) for c in range(tk // sub)]
    m_all = m_s[...]
    l_all = l_s[...]
    tqs = tq // n_bias
    for n, bias_ref in enumerate(bias_refs):
        bias_s[:, n * tqs:(n + 1) * tqs] = bias_ref[0, 0, 0].astype(F32)

    def stage_a(hh, keys, mx):
        s = jnp.dot(k_ref[0, hh, keys, :], qt_ref[0, hh * HEAD_DIM:(hh + 1) * HEAD_DIM, :],
                    preferred_element_type=F32) + bias_s[keys, :]
        s_s[hh % 2, keys, :] = s
        return jnp.maximum(mx, jnp.max(s, axis=0, keepdims=True))

    def stage_b(hh, keys, m_new, lsum, pv):
        p = jnp.exp2(s_s[hh % 2, keys, :] - m_new)
        lsum = lsum + jnp.sum(p, axis=0, keepdims=True)
        pv = pv + jnp.dot(vt_ref[0, hh * HEAD_DIM:(hh + 1) * HEAD_DIM, keys], p.astype(MXU_DTYPE),
                          preferred_element_type=F32)
        return lsum, pv

    m_new = m_all[0:1, :]
    for keys in chunks:
        m_new = stage_a(0, keys, m_new)
    m_out, l_out = [], []
    for hh in range(N_HEADS):
        rows = slice(hh * HEAD_DIM, (hh + 1) * HEAD_DIM)
        lsum = jnp.zeros((1, tq), F32)
        pv = jnp.zeros((HEAD_DIM, tq), F32)
        m_next = m_all[hh + 1:hh + 2, :] if hh + 1 < N_HEADS else None
        for keys in chunks:
            if m_next is not None:
                m_next = stage_a(hh + 1, keys, m_next)
            lsum, pv = stage_b(hh, keys, m_new, lsum, pv)
        alpha = jnp.exp2(m_all[hh:hh + 1, :] - m_new)
        l_out.append(alpha * l_all[hh:hh + 1, :] + lsum)
        acc_s[rows, :] = alpha * acc_s[rows, :] + pv
        m_out.append(m_new)
        m_new = m_next
    m_s[...] = jnp.concatenate(m_out, axis=0)
    l_s[...] = jnp.concatenate(l_out, axis=0)

    @pl.when(j == last)
    def _():
        out_t = jnp.concatenate(
            [acc_s[hh * HEAD_DIM:(hh + 1) * HEAD_DIM, :] / l_s[hh:hh + 1, :] for hh in range(N_HEADS)], axis=0)
        o_ref[0] = out_t.T


def _attn_chunk(tq, tk):
    lane_tiles = -(-tq // LANES)
    for sub in (256, 128, 64, 32, 16, 8):
        if tk % sub == 0 and (sub // SUBLANES) * lane_tiles <= 32:
            return sub
    raise AssertionError((tq, tk))


def _attention(qt, k_hm, vt, bias, tq, tk, causal):
    b, width, t = qt.shape
    s_len = vt.shape[-1]
    tqs = bias.shape[-1]
    n_bias = tq // tqs
    nq, nkb = t // tq, s_len // tk
    pairs = [(i, j) for i in range(nq) for j in range(_num_key_blocks(i, tq, tk, nkb, causal))]
    qtile = jnp.asarray([p[0] for p in pairs], I32)
    kblock = jnp.asarray([p[1] for p in pairs], I32)

    def bias_spec(n):
        return pl.BlockSpec((1, 1, 1, tk, tqs), lambda bb, s, qi, kj: (bb, qi[s] * n_bias + n, kj[s], 0, 0))

    grid_spec = pltpu.PrefetchScalarGridSpec(
        num_scalar_prefetch=2,
        grid=(b, len(pairs)),
        in_specs=[pl.BlockSpec((1, width, tq), lambda bb, s, qi, kj: (bb, 0, qi[s])),
                  pl.BlockSpec((1, N_HEADS, tk, HEAD_DIM), lambda bb, s, qi, kj: (bb, 0, kj[s], 0)),
                  pl.BlockSpec((1, width, tk), lambda bb, s, qi, kj: (bb, 0, kj[s]))]
                 + [bias_spec(n) for n in range(n_bias)],
        out_specs=pl.BlockSpec((1, tq, width), lambda bb, s, qi, kj: (bb, qi[s], 0)),
        scratch_shapes=[pltpu.VMEM((N_HEADS, tq), F32), pltpu.VMEM((N_HEADS, tq), F32),
                        pltpu.VMEM((width, tq), F32), pltpu.VMEM((2, tk, tq), F32),
                        pltpu.VMEM((tk, tq), F32)])
    return pl.pallas_call(
        functools.partial(_attn_kernel, tq=tq, tk=tk, nkb=nkb, causal=causal, n_bias=n_bias),
        grid_spec=grid_spec,
        out_shape=jax.ShapeDtypeStruct((b, t, width), F32),
        compiler_params=_cparams(("arbitrary", "arbitrary")),
        name="masked_attention",
    )(qtile, kblock, qt, k_hm, vt, *([bias] * n_bias))


def _merge_kernel(x_ref, ys_ref, att_ref, za_ref, gs_ref, ga_ref, gate_ref, wps_ref, wpa_ref, wo_ref,
                  gfin_ref, o_ref, *, final_norm):
    ya = att_ref[0] * _silu(za_ref[0])
    merged = (jax.nn.sigmoid(gs_ref[0]) * _mxu_dot(ys_ref[...], wps_ref[...])
              + jax.nn.sigmoid(ga_ref[0]) * _mxu_dot(ya, wpa_ref[...]))
    x_new = x_ref[0] + gate_ref[0] * _mxu_dot(merged, wo_ref[...])
    if final_norm:
        x_new = x_new * lax.rsqrt(jnp.mean(x_new * x_new, axis=-1, keepdims=True) + EPS) * gfin_ref[...]
    o_ref[0] = x_new


def _merge(x, ys, att, za, gs, ga, gate, w_ps, w_pa, w_o, g_final, tm, final_norm):
    b, t, d = x.shape
    half = d // 2
    row = lambda width: pl.BlockSpec((1, tm, width), lambda i, j: (i, j, 0))
    const = lambda shape: pl.BlockSpec(shape, lambda i, j: (0,) * len(shape))
    return pl.pallas_call(
        functools.partial(_merge_kernel, final_norm=final_norm),
        grid=(b, t // tm),
        in_specs=[row(d), pl.BlockSpec((tm, half), lambda i, j: (j, i)), row(half), row(half), row(d), row(d),
                  pl.BlockSpec((1, 1, d), lambda i, j: (i, 0, 0)),
                  const((half, d)), const((half, d)), const((d, d)), const((1, d))],
        out_specs=row(d),
        out_shape=jax.ShapeDtypeStruct((b, t, d), F32),
        compiler_params=_cparams(("arbitrary", "arbitrary")),
        name="merge_out",
    )(x, ys, att, za, gs, ga, gate, w_ps.astype(MXU_DTYPE), w_pa.astype(MXU_DTYPE),
      w_o.astype(MXU_DTYPE), g_final.reshape(1, d))


def _row_tile(t, cap):
    tile = min(t, cap)
    assert t % tile == 0, (t, tile)
    return tile


def _key_blocks(s_len):
    s_pad = -(-s_len // LANES) * LANES
    for tk in (512, 384, 256, 128):
        if s_pad % tk == 0:
            return s_pad, tk
    raise AssertionError(s_len)


def _pad_keys(a, s_pad, axis):
    pad = s_pad - a.shape[axis]
    if pad == 0:
        return a
    widths = [(0, 0)] * a.ndim
    widths[axis] = (0, pad)
    return jnp.pad(a, widths)


def _layer(x, mod, lw, ssm_par, h0_re, h0_im, past, final_norm):
    b, t, d = x.shape
    shift, scale, gate = (m.reshape(b, 1, d) for m in jnp.split(mod, 3, axis=-1))
    tm = _row_tile(t, 256)
    (u, zs, za, gs, ga, k, v, ki, wit, qt, qit, vt, k_hm) = _inproj(
        x, shift, scale, lw["g_norm"], lw["wmain"], lw["wsmall"], tm)

    wb, wc, coef = ssm_par
    half = d // 2
    ys, hr, hi = _ssm(u.reshape(t * b, half), zs.reshape(t * b, half), b, h0_re, h0_im, wb, wc, coef,
                      lw["d_skip"], lw["w_glu"], lw["b_glu"], min(t * b, 256))
    ys = ys.reshape(t, b * half)

    ki_bf = ki.astype(MXU_DTYPE)
    if past is None:
        causal, n_valid = True, t
    else:
        past_k, past_v, past_ki = past
        causal, n_valid = False, past_k.shape[1] + t
        k_hm = jnp.concatenate([past_k.transpose(0, 2, 1, 3).astype(MXU_DTYPE), k_hm], axis=2)
        vt = jnp.concatenate([past_v.reshape(b, -1, d // 2).transpose(0, 2, 1).astype(MXU_DTYPE), vt], axis=2)
        ki_bf = jnp.concatenate([past_ki.astype(MXU_DTYPE), ki_bf], axis=1)
    n_sel = min(TOPK_MAX, n_valid // 4)
    s_pad, tk = _key_blocks(n_valid)
    k_hm = _pad_keys(k_hm, s_pad, 2)
    vt = _pad_keys(vt, s_pad, 2)
    ki_blk = _pad_keys(ki_bf, s_pad, 1).reshape(b, s_pad // tk, tk, IDX_DIM)

    tq = _row_tile(t, 256)
    bias = _select(qit, wit, ki_blk, tq, causal, n_valid, n_sel)
    att = _attention(qt, k_hm, vt, bias, tq, tk, causal)

    x_new = _merge(x, ys, att, za, gs, ga, gate, lw["w_ps"], lw["w_pa"], lw["w_o"], lw["g_final"],
                   _row_tile(t, 256), final_norm)
    gshape = (b, -1, STATE)
    return x_new, (k.reshape(b, t, N_HEADS, HEAD_DIM), v.reshape(b, t, N_HEADS, HEAD_DIM), ki,
                   hr.reshape(gshape), hi.reshape(gshape))


def kernel(x_prompt, x_sample, cache_k, cache_v, cache_kidx, state_ssm_re, state_ssm_im, c_prompt, c_sample, w_mod, b_mod, g_norm, w_in, a_re, a_im, log_dt, b_re, b_im, c_re, c_im, d_skip, w_glu, b_glu, w_ps, w_pa, w_o, g_final):
    depth = w_mod.shape[0]
    bp, _, d = x_prompt.shape
    g, p = a_re.shape[1:]
    mod = _modulation(jnp.concatenate([c_prompt, c_sample], axis=0), w_mod, b_mod)
    zeros = jnp.zeros((bp, g * p), F32)
    xp, xs = x_prompt, x_sample
    outs_p, outs_s = [], []
    for l in range(depth):
        wmain, wsmall = _inproj_weights(w_in[l], d)
        lw = dict(g_norm=g_norm[l], wmain=wmain, wsmall=wsmall, d_skip=d_skip[l], w_glu=w_glu[l],
                  b_glu=b_glu[l], w_ps=w_ps[l], w_pa=w_pa[l], w_o=w_o[l], g_final=g_final)
        ssm_par = _ssm_prep(a_re[l], a_im[l], log_dt[l], b_re[l], b_im[l], c_re[l], c_im[l])
        last = l == depth - 1
        xp, op = _layer(xp, mod[l, :bp], lw, ssm_par, zeros, zeros, None, last)
        xs, os_ = _layer(xs, mod[l, bp:], lw, ssm_par, state_ssm_re[l].reshape(-1, g * p),
                         state_ssm_im[l].reshape(-1, g * p), (cache_k[l], cache_v[l], cache_kidx[l]), last)
        outs_p.append(op)
        outs_s.append(os_)
    stack = lambda outs: tuple(jnp.stack(z) for z in zip(*outs))
    return (xp, xs) + stack(outs_p) + stack(outs_s)
```

```python
import functools
import math

import jax
import jax.numpy as jnp
from jax import lax
from jax.experimental import pallas as pl
from jax.experimental.pallas import tpu as pltpu

F32 = jnp.float32
I32 = jnp.int32
MXU_DTYPE = jnp.bfloat16

CHUNK = 64
SSM_GROUP = 16
STATE = 64
N_HEADS = 8
HEAD_DIM = 64
IDX_HEADS = 8
IDX_DIM = 64
TOPK_MAX = 256
EPS = 1e-6
LOG2E = math.log2(math.e)

LANES = 128
SUBLANES = 8
V7X_VMEM_BYTES = 64 * 1024 * 1024
VMEM_LIMIT = V7X_VMEM_BYTES - 8 * 1024 * 1024

INT_MIN = -(2 ** 31)
INT16_MIN = -(2 ** 15)
MASK_NEG = -1e30
M_INIT = -1e29
NO_TIE_LIMIT = 2 ** 30
COUNT_ROWS = 32


def _cparams(sem):
    return pltpu.CompilerParams(dimension_semantics=sem, vmem_limit_bytes=VMEM_LIMIT)


def _silu(x):
    return x * jax.nn.sigmoid(x)


def _mxu_dot(a, b):
    return jnp.dot(a.astype(MXU_DTYPE), b.astype(MXU_DTYPE), preferred_element_type=F32)


def _split_hi_lo(a):
    hi = a.astype(jnp.bfloat16)
    lo = (a - hi.astype(F32)).astype(jnp.bfloat16)
    return hi, lo


def _dot_3pass(a, b):
    ah, al = _split_hi_lo(a)
    bh, bl = _split_hi_lo(b)
    d = functools.partial(jnp.dot, preferred_element_type=F32)
    return d(ah, bh) + (d(ah, bl) + d(al, bh))


def _mod_kernel(c_ref, w_ref, b_ref, o_ref):
    o_ref[...] = _dot_3pass(_silu(c_ref[...]), w_ref[...]) + b_ref[...]


def _modulation(c_all, w_mod, b_mod):
    depth, d, d3 = w_mod.shape
    rows = c_all.shape[0]
    cols = d3 // 4
    return pl.pallas_call(
        _mod_kernel,
        grid=(depth, d3 // cols),
        in_specs=[
            pl.BlockSpec((rows, d), lambda l, j: (0, 0)),
            pl.BlockSpec((None, d, cols), lambda l, j: (l, 0, j)),
            pl.BlockSpec((None, 1, cols), lambda l, j: (l, 0, j)),
        ],
        out_specs=pl.BlockSpec((None, rows, cols), lambda l, j: (l, 0, j)),
        out_shape=jax.ShapeDtypeStruct((depth, rows, d3), F32),
        compiler_params=_cparams(("arbitrary", "arbitrary")),
        name="adaln_mod",
    )(c_all, w_mod, b_mod.reshape(depth, 1, d3))


def _ssm_prep_kernel(are_ref, aim_ref, ldt_ref, bre_ref, bim_ref, cre_ref, cim_ref,
                     wb_ref, wc_ref, coef_ref):
    a_re = are_ref[...]
    a_im = aim_ref[...]
    dt = jnp.exp(ldt_ref[...])
    gp = a_re.shape[-1]

    mag = jnp.exp(a_re * dt)
    ang = a_im * dt
    ab_re, ab_im = mag * jnp.cos(ang), mag * jnp.sin(ang)
    den = a_re * a_re + a_im * a_im
    nr = ab_re - 1.0
    ni = ab_im
    f_re = (nr * a_re + ni * a_im) / den
    f_im = (ni * a_re - nr * a_im) / den

    rows_b = lax.broadcasted_iota(I32, bre_ref.shape, 0) // SSM_GROUP
    cols_b = lax.broadcasted_iota(I32, bre_ref.shape, 1) // STATE
    on_b = rows_b == cols_b
    b_re = bre_ref[...]
    b_im = bim_ref[...]
    wb_ref[:, 0:gp] = jnp.where(on_b, f_re * b_re - f_im * b_im, 0.0).astype(wb_ref.dtype)
    wb_ref[:, gp:2 * gp] = jnp.where(on_b, f_re * b_im + f_im * b_re, 0.0).astype(wb_ref.dtype)

    rows_c = lax.broadcasted_iota(I32, cre_ref.shape, 0) // STATE
    cols_c = lax.broadcasted_iota(I32, cre_ref.shape, 1) // SSM_GROUP
    on_c = rows_c == cols_c
    wc_ref[0:gp, :] = jnp.where(on_c, cre_ref[...], 0.0).astype(wc_ref.dtype)
    wc_ref[gp:2 * gp, :] = jnp.where(on_c, -cim_ref[...], 0.0).astype(wc_ref.dtype)

    coef_ref[0] = jnp.broadcast_to(ab_re, (SUBLANES, gp))
    coef_ref[1] = jnp.broadcast_to(ab_im, (SUBLANES, gp))


def _ssm_prep(a_re, a_im, log_dt, b_re, b_im, c_re, c_im):
    g, p = a_re.shape
    n = b_re.shape[-1]
    gp, gn = g * p, g * n
    row = lambda a: a.reshape(1, gp)
    ldt = jnp.broadcast_to(log_dt[:, None], (g, p)).reshape(1, gp)
    b_exp = lambda b: jnp.broadcast_to(b.transpose(2, 0, 1).reshape(1, n, gp), (g, n, gp)).reshape(gn, gp)
    c_exp = lambda c: jnp.broadcast_to(c.transpose(0, 2, 1).reshape(gp, 1, n), (gp, g, n)).reshape(gp, gn)
    return pl.pallas_call(
        _ssm_prep_kernel,
        out_shape=(jax.ShapeDtypeStruct((gn, 2 * gp), MXU_DTYPE),
                   jax.ShapeDtypeStruct((2 * gp, gn), MXU_DTYPE),
                   jax.ShapeDtypeStruct((2, SUBLANES, gp), F32)),
        compiler_params=pltpu.CompilerParams(vmem_limit_bytes=VMEM_LIMIT),
        name="ssm_prep",
    )(row(a_re), row(a_im), ldt, b_exp(b_re), b_exp(b_im), c_exp(c_re), c_exp(c_im))


_MAIN_SEGS = ("u", "zs", "za", "gs", "ga", "k", "v", "q", "qi")


def _main_offsets(d):
    half = d // 2
    widths = dict(u=half, zs=half, za=half, gs=d, ga=d, k=half, v=half, q=half, qi=half)
    offs, o = {}, 0
    for name in _MAIN_SEGS:
        offs[name] = (o, o + widths[name])
        o += widths[name]
    return offs, o


def _inproj_kernel(x_ref, shift_ref, scale_ref, g_ref, wmain_ref, wsmall_ref,
                   u_ref, zs_ref, za_ref, gs_ref, ga_ref, k_ref, v_ref, ki_ref,
                   wit_ref, qt_ref, qit_ref, vt_ref, khm_ref, *, offs):
    x = x_ref[0]
    h = x * lax.rsqrt(jnp.mean(x * x, axis=-1, keepdims=True) + EPS) * g_ref[...]
    h = h * (1.0 + scale_ref[0]) + shift_ref[0]
    hb = h.astype(MXU_DTYPE)

    def proj(name):
        c0, c1 = offs[name]
        return jnp.dot(hb, wmain_ref[:, c0:c1], preferred_element_type=F32)

    u_ref[...] = proj("u")
    zs_ref[...] = proj("zs")
    za_ref[0] = proj("za")
    gs_ref[0] = proj("gs")
    ga_ref[0] = proj("ga")

    zk = proj("k")
    k_ref[0] = zk
    for hh in range(N_HEADS):
        khm_ref[0, hh] = zk[:, hh * HEAD_DIM:(hh + 1) * HEAD_DIM].astype(khm_ref.dtype)
    zv = proj("v")
    v_ref[0] = zv
    vt_ref[0] = zv.T.astype(vt_ref.dtype)
    qt_ref[0] = proj("q").T.astype(qt_ref.dtype)
    qit_ref[0] = proj("qi").T.astype(qit_ref.dtype)

    zsm = jnp.dot(hb, wsmall_ref[...], preferred_element_type=F32)
    ki_ref[0] = zsm[:, 0:IDX_DIM]
    wit_ref[0] = zsm.T[IDX_DIM:IDX_DIM + IDX_HEADS, :] * (IDX_HEADS ** -0.5)


def _inproj_weights(w_in, d):
    half = d // 2
    idx_w = IDX_HEADS * IDX_DIM
    o = 0
    seg = {}
    for name, width in (("u", half), ("zs", half), ("q", half), ("k", half), ("v", half), ("za", half),
                        ("qi", idx_w), ("ki", IDX_DIM), ("wi", IDX_HEADS), ("gs", d), ("ga", d)):
        seg[name] = w_in[:, o:o + width]
        o += width
    seg["q"] = seg["q"] * (HEAD_DIM ** -0.5 * LOG2E)
    seg["qi"] = seg["qi"] * (IDX_DIM ** -0.5)
    wmain = jnp.concatenate([seg[n] for n in _MAIN_SEGS], axis=1).astype(MXU_DTYPE)
    wsmall = jnp.concatenate(
        [seg["ki"], seg["wi"], jnp.zeros((d, LANES - IDX_DIM - IDX_HEADS), w_in.dtype)], axis=1).astype(MXU_DTYPE)
    return wmain, wsmall


def _inproj(x, shift, scale, g_norm, wmain, wsmall, tm):
    b, t, d = x.shape
    half = d // 2
    offs, wcols = _main_offsets(d)
    nt = t // tm
    row = lambda width: pl.BlockSpec((1, tm, width), lambda i, j: (i, j, 0))
    hm = pl.BlockSpec((1, N_HEADS, tm, HEAD_DIM), lambda i, j: (i, 0, j, 0))
    tr = lambda rows: pl.BlockSpec((1, rows, tm), lambda i, j: (i, 0, j))
    tb = pl.BlockSpec((tm, half), lambda i, j: (j, i))
    per_b = pl.BlockSpec((1, 1, d), lambda i, j: (i, 0, 0))
    const = lambda shape: pl.BlockSpec(shape, lambda i, j: (0,) * len(shape))
    f = lambda *s: jax.ShapeDtypeStruct(s, F32)
    bf = lambda *s: jax.ShapeDtypeStruct(s, MXU_DTYPE)
    return pl.pallas_call(
        functools.partial(_inproj_kernel, offs=offs),
        grid=(b, nt),
        in_specs=[row(d), per_b, per_b, const((1, d)), const((d, wcols)), const((d, LANES))],
        out_specs=[tb, tb, row(half), row(d), row(d), row(half), row(half),
                   row(IDX_DIM), tr(IDX_HEADS), tr(half), tr(half), tr(half), hm],
        out_shape=[f(t, b * half), f(t, b * half), f(b, t, half), f(b, t, d), f(b, t, d),
                   f(b, t, half), f(b, t, half), f(b, t, IDX_DIM), f(b, IDX_HEADS, t),
                   bf(b, half, t), bf(b, half, t), bf(b, half, t), bf(b, N_HEADS, t, HEAD_DIM)],
        compiler_params=_cparams(("arbitrary", "arbitrary")),
        name="inproj",
    )(x, shift, scale, g_norm.reshape(1, d), wmain, wsmall)


def _ssm_kernel(u_ref, zs_ref, h0r_ref, h0i_ref, wb_ref, wc_ref, coef_ref, d_ref, wglu_ref, bglu_ref,
                y_ref, hr_ref, hi_ref, bu_s, u_s, y_s, *, tt, nb, gp):
    @pl.when(pl.program_id(0) == 0)
    def _():
        hr_ref[...] = h0r_ref[...]
        hi_ref[...] = h0i_ref[...]

    slabs = u_s.shape[0]
    dn = slabs * LANES
    for bb in range(nb):
        for j in range(slabs):
            lanes = slice(bb * dn + j * LANES, bb * dn + (j + 1) * LANES)
            u_s[j, pl.ds(bb, tt, stride=nb), :] = u_ref[:, lanes]
    u = jnp.concatenate([u_s[j] for j in range(slabs)], axis=-1)
    ub = u.astype(MXU_DTYPE)
    sw = gp // slabs
    for j in range(slabs):
        ch = slice(j * LANES, (j + 1) * LANES)
        for part in range(2):
            st = slice(part * gp + j * sw, part * gp + (j + 1) * sw)
            bu_s[:, st] = jnp.dot(ub[:, ch], wb_ref[ch, st], preferred_element_type=F32)

    groups = nb // SUBLANES

    def step(t, carry):
        a_re = coef_ref[0]
        a_im = coef_ref[1]
        out = []
        for g in range(groups):
            h_re, h_im = carry[2 * g], carry[2 * g + 1]
            rows = pl.ds(pl.multiple_of(t * nb + g * SUBLANES, SUBLANES), SUBLANES)
            n_re = (a_re * h_re - a_im * h_im) + bu_s[rows, 0:gp]
            n_im = (a_re * h_im + a_im * h_re) + bu_s[rows, gp:2 * gp]
            bu_s[rows, 0:gp] = n_re
            bu_s[rows, gp:2 * gp] = n_im
            out += [n_re, n_im]
        return tuple(out)

    init = []
    for g in range(groups):
        rows = slice(g * SUBLANES, (g + 1) * SUBLANES)
        init += [hr_ref[rows, :], hi_ref[rows, :]]
    fin = lax.fori_loop(0, tt, step, tuple(init), unroll=2)
    for g in range(groups):
        rows = slice(g * SUBLANES, (g + 1) * SUBLANES)
        hr_ref[rows, :] = fin[2 * g]
        hi_ref[rows, :] = fin[2 * g + 1]

    ys = []
    for j in range(slabs):
        ch = slice(j * LANES, (j + 1) * LANES)
        acc = d_ref[:, ch] * u[:, ch]
        for part in range(2):
            st = slice(part * gp + j * sw, part * gp + (j + 1) * sw)
            acc = acc + jnp.dot(bu_s[:, st].astype(MXU_DTYPE), wc_ref[st, ch], preferred_element_type=F32)
        ys.append(acc)
    y = jax.nn.gelu(jnp.concatenate(ys, axis=-1))
    y = y * jax.nn.sigmoid(_mxu_dot(y, wglu_ref[...]) + bglu_ref[...])
    for j in range(slabs):
        y_s[j] = y[:, j * LANES:(j + 1) * LANES]
    for bb in range(nb):
        for j in range(slabs):
            lanes = slice(bb * dn + j * LANES, bb * dn + (j + 1) * LANES)
            y_ref[:, lanes] = y_s[j, pl.ds(bb, tt, stride=nb), :] * _silu(zs_ref[:, lanes])


def _ssm(u, zs, nb, h0_re, h0_im, wb, wc, coef, d_skip, w_glu, b_glu, rows_per_step):
    t, width = u.shape
    dn = width // nb
    gp = wb.shape[1] // 2
    assert nb % SUBLANES == 0 and rows_per_step % nb == 0
    tt = rows_per_step // nb
    assert t % tt == 0
    row = pl.BlockSpec((tt, width), lambda j: (j, 0))
    const = lambda shape: pl.BlockSpec(shape, lambda j: (0,) * len(shape))
    return pl.pallas_call(
        functools.partial(_ssm_kernel, tt=tt, nb=nb, gp=gp),
        grid=(t // tt,),
        in_specs=[row, row, const((nb, gp)), const((nb, gp)), const(wb.shape), const(wc.shape),
                  const(coef.shape), const((1, dn)), const(w_glu.shape), const((1, dn))],
        out_specs=[row, const((nb, gp)), const((nb, gp))],
        out_shape=[jax.ShapeDtypeStruct((t, width), F32),
                   jax.ShapeDtypeStruct((nb, gp), F32),
                   jax.ShapeDtypeStruct((nb, gp), F32)],
        scratch_shapes=[pltpu.VMEM((rows_per_step, 2 * gp), F32),
                        pltpu.VMEM((dn // LANES, rows_per_step, LANES), F32),
                        pltpu.VMEM((dn // LANES, rows_per_step, LANES), F32)],
        compiler_params=_cparams(("arbitrary",)),
        name="ssm_scan",
    )(u, zs, h0_re, h0_im, wb, wc, coef,
      d_skip.reshape(1, dn), w_glu.astype(MXU_DTYPE), b_glu.reshape(1, dn))


def _num_key_blocks(i, tq, tk, nkb, causal):
    if not causal:
        return nkb
    return ((i + 1) * tq + (tk - 1)) // tk


def _select_kernel(qit_ref, wit_ref, ki_ref, bias_ref, keys_s, hi_s, lo_s, *,
                   tq, tk, nkb, causal, n_valid, n_sel):
    i = pl.program_id(1)
    nb = _num_key_blocks(i, tq, tk, nkb, causal)
    if causal:
        q_t = i * tq + lax.broadcasted_iota(I32, (1, tq), 1)
        limit = (q_t // CHUNK + 1) * CHUNK
    else:
        limit = n_valid
    wit = wit_ref[0]

    def key_index(j):
        return j * tk + lax.broadcasted_iota(I32, (tk, tq), 0)

    def score_block(j, c):
        kb = ki_ref[0, j]
        s = jnp.zeros((tk, tq), F32)
        for hh in range(IDX_HEADS):
            lg = jnp.dot(kb, qit_ref[0, hh * IDX_DIM:(hh + 1) * IDX_DIM, :], preferred_element_type=F32)
            s = s + wit[hh:hh + 1, :] * jnp.maximum(lg, 0.0)
        bits = pltpu.bitcast(s, I32)
        bits = jnp.where(bits == INT_MIN, 0, bits)
        key = bits ^ ((bits >> 31) & 0x7FFFFFFF)
        key = jnp.where(key_index(j) < limit, key, INT_MIN)
        keys_s[j] = key
        hi_s[j] = (key >> 16).astype(jnp.int16)
        return c

    lax.fori_loop(0, nb, score_block, 0)

    def count(pred_fns):
        def body(j, accs):
            key = keys_s[j]
            return tuple(acc + jnp.sum(jnp.where(fn(key), 1.0, 0.0).reshape(tk // COUNT_ROWS, COUNT_ROWS, tq),
                                       axis=0)
                         for fn, acc in zip(pred_fns, accs))
        accs = lax.fori_loop(0, nb, body, tuple(jnp.zeros((COUNT_ROWS, tq), F32) for _ in pred_fns))
        return [jnp.sum(a, axis=0, keepdims=True) for a in accs]

    def count16(ref, thr16, strict):
        def body(j, acc):
            blk = ref[j]
            hit = blk > thr16 if strict else blk >= thr16
            ind = jnp.where(hit, jnp.int16(1), jnp.int16(0))
            part = ind[0:COUNT_ROWS]
            for r in range(1, tk // COUNT_ROWS):
                part = part + ind[r * COUNT_ROWS:(r + 1) * COUNT_ROWS]
            return acc + part.astype(F32)
        acc = lax.fori_loop(0, nb, body, jnp.zeros((COUNT_ROWS, tq), F32))
        return jnp.sum(acc, axis=0, keepdims=True)

    def radix16(ref, need, count_all):
        def bit_step(bi, carry):
            u, c_acc = carry
            cand = u | lax.shift_left(jnp.int32(1), 15 - bi)
            cnt = count16(ref, (cand + INT16_MIN).astype(jnp.int16), False)
            keep = cnt >= need
            return jnp.where(keep, cand, u), jnp.where(keep, cnt, c_acc)
        u, c_acc = lax.fori_loop(0, 16, bit_step, (jnp.zeros((1, tq), I32), count_all))
        return u + INT16_MIN, c_acc

    n_adm = jnp.broadcast_to(jnp.asarray(limit, F32), (1, tq))
    n_all = jnp.broadcast_to((nb * tk).astype(F32) if causal else jnp.float32(nb * tk), (1, tq))
    hi_sel, c_ge_hi = radix16(hi_s, n_sel, n_all)
    c_gt_hi = count16(hi_s, hi_sel.astype(jnp.int16), True)
    want_lo = n_sel - c_gt_hi

    def low_block(j, c):
        key = keys_s[j]
        low = (key & 0xFFFF) + INT16_MIN
        lo_s[j] = jnp.where((key >> 16) == hi_sel, low, INT16_MIN).astype(jnp.int16)
        return c

    lax.fori_loop(0, nb, low_block, 0)
    lo_sel, c_ge_lo = radix16(lo_s, want_lo, c_ge_hi - c_gt_hi)
    kth = lax.shift_left(hi_sel, 16) | (lo_sel - INT16_MIN)
    c_ge = jnp.where(kth == INT_MIN, n_adm, c_gt_hi + c_ge_lo)
    thr = jnp.maximum(kth, INT_MIN + 1)
    tie_q = c_ge > n_sel
    has_tie = jnp.max(jnp.where(tie_q, 1.0, 0.0)) > 0.0

    @pl.when(jnp.logical_not(has_tie))
    def _():
        def bias_block(j, c):
            bias_ref[0, 0, j] = jnp.where(keys_s[j] >= thr, 0.0, MASK_NEG).astype(bias_ref.dtype)
            return c

        lax.fori_loop(0, nb, bias_block, 0)

    @pl.when(has_tie)
    def _():
        (c_gt,) = count([lambda kk: kk > thr])
        want = n_sel - c_gt
        tri = jnp.where(lax.broadcasted_iota(I32, (tk, tk), 0) >= lax.broadcasted_iota(I32, (tk, tk), 1),
                        1.0, 0.0).astype(jnp.bfloat16)

        def body(j, carry):
            before, below = carry
            eq = jnp.where(keys_s[j] == thr, 1.0, 0.0).astype(jnp.bfloat16)
            pref = before + jnp.dot(tri, eq, preferred_element_type=F32)
            below = below + jnp.sum(jnp.where(pref < want, 1.0, 0.0), axis=0, keepdims=True)
            return pref[tk - 1:tk, :], below

        _, below = lax.fori_loop(0, nb, body, (jnp.zeros((1, tq), F32), jnp.zeros((1, tq), F32)))
        jstar = jnp.where(tie_q, below.astype(I32), NO_TIE_LIMIT)

        def bias_block(j, c):
            key = keys_s[j]
            tie_ok = jnp.where(key_index(j) <= jstar, 0.0, MASK_NEG)
            bias = jnp.where(key > thr, 0.0, jnp.where(key == thr, tie_ok, MASK_NEG))
            bias_ref[0, 0, j] = bias.astype(bias_ref.dtype)
            return c

        lax.fori_loop(0, nb, bias_block, 0)

    def masked_block(j, c):
        bias_ref[0, 0, j] = jnp.full((tk, tq), MASK_NEG, bias_ref.dtype)
        return c

    lax.fori_loop(nb, nkb, masked_block, 0)


def _select(qit, wit, ki_blk, tq, causal, n_valid, n_sel):
    b, width, t = qit.shape
    _, nkb, tk, _ = ki_blk.shape
    nq = t // tq
    return pl.pallas_call(
        functools.partial(_select_kernel, tq=tq, tk=tk, nkb=nkb, causal=causal, n_valid=n_valid, n_sel=n_sel),
        grid=(b, nq),
        in_specs=[pl.BlockSpec((1, width, tq), lambda i, j: (i, 0, j)),
                  pl.BlockSpec((1, IDX_HEADS, tq), lambda i, j: (i, 0, j)),
                  pl.BlockSpec((1, nkb, tk, IDX_DIM), lambda i, j: (i, 0, 0, 0))],
        out_specs=pl.BlockSpec((1, 1, nkb, tk, tq), lambda i, j: (i, j, 0, 0, 0)),
        out_shape=jax.ShapeDtypeStruct((b, nq, nkb, tk, tq), jnp.bfloat16),
        scratch_shapes=[pltpu.VMEM((nkb, tk, tq), I32), pltpu.VMEM((nkb, tk, tq), jnp.int16),
                        pltpu.VMEM((nkb, tk, tq), jnp.int16)],
        compiler_params=_cparams(("arbitrary", "arbitrary")),
        name="topk_select",
    )(qit, wit, ki_blk)


def _attn_kernel(qtile_ref, kblock_ref, qt_ref, k_ref, vt_ref, *rest, tq, tk, nkb, causal, n_bias):
    bias_refs = rest[:n_bias]
    o_ref, m_s, l_s, acc_s, s_s, bias_s = rest[n_bias:]
    step = pl.program_id(1)
    j = kblock_ref[step]
    last = _num_key_blocks(qtile_ref[step], tq, tk, nkb, causal) - 1

    @pl.when(j == 0)
    def _():
        m_s[...] = jnp.full(m_s.shape, M_INIT, F32)
        l_s[...] = jnp.zeros(l_s.shape, F32)
        acc_s[...] = jnp.zeros(acc_s.shape, F32)

    sub = _attn_chunk(tq, tk)
    chunks = [slice(c * sub, (c + 1) * sub) for c in range(tk // sub)]
    m_all = m_s[...]
    l_all = l_s[...]
    tqs = tq // n_bias
    for n, bias_ref in enumerate(bias_refs):
        bias_s[:, n * tqs:(n + 1) * tqs] = bias_ref[0, 0, 0].astype(F32)

    def stage_a(hh, keys, mx):
        s = jnp.dot(k_ref[0, hh, keys, :], qt_ref[0, hh * HEAD_DIM:(hh + 1) * HEAD_DIM, :],
                    preferred_element_type=F32) + bias_s[keys, :]
        s_s[hh % 2, keys, :] = s
        return jnp.maximum(mx, jnp.max(s, axis=0, keepdims=True))

    def stage_b(hh, keys, m_new, lsum, pv):
        p = jnp.exp2(s_s[hh % 2, keys, :] - m_new)
        lsum = lsum + jnp.sum(p, axis=0, keepdims=True)
        pv = pv + jnp.dot(vt_ref[0, hh * HEAD_DIM:(hh + 1) * HEAD_DIM, keys], p.astype(MXU_DTYPE),
                          preferred_element_type=F32)
        return lsum, pv

    m_new = m_all[0:1, :]
    for keys in chunks:
        m_new = stage_a(0, keys, m_new)
    m_out, l_out = [], []
    for hh in range(N_HEADS):
        rows = slice(hh * HEAD_DIM, (hh + 1) * HEAD_DIM)
        lsum = jnp.zeros((1, tq), F32)
        pv = jnp.zeros((HEAD_DIM, tq), F32)
        m_next = m_all[hh + 1:hh + 2, :] if hh + 1 < N_HEADS else None
        for keys in chunks:
            if m_next is not None:
                m_next = stage_a(hh + 1, keys, m_next)
            lsum, pv = stage_b(hh, keys, m_new, lsum, pv)
        alpha = jnp.exp2(m_all[hh:hh + 1, :] - m_new)
        l_out.append(alpha * l_all[hh:hh + 1, :] + lsum)
        acc_s[rows, :] = alpha * acc_s[rows, :] + pv
        m_out.append(m_new)
        m_new = m_next
    m_s[...] = jnp.concatenate(m_out, axis=0)
    l_s[...] = jnp.concatenate(l_out, axis=0)

    @pl.when(j == last)
    def _():
        out_t = jnp.concatenate(
            [acc_s[hh * HEAD_DIM:(hh + 1) * HEAD_DIM, :] / l_s[hh:hh + 1, :] for hh in range(N_HEADS)], axis=0)
        o_ref[0] = out_t.T


def _attn_chunk(tq, tk):
    lane_tiles = -(-tq // LANES)
    for sub in (256, 128, 64, 32, 16, 8):
        if tk % sub == 0 and (sub // SUBLANES) * lane_tiles <= 32:
            return sub
    raise AssertionError((tq, tk))


def _attention(qt, k_hm, vt, bias, tq, tk, causal):
    b, width, t = qt.shape
    s_len = vt.shape[-1]
    tqs = bias.shape[-1]
    n_bias = tq // tqs
    nq, nkb = t // tq, s_len // tk
    pairs = [(i, j) for i in range(nq) for j in range(_num_key_blocks(i, tq, tk, nkb, causal))]
    qtile = jnp.asarray([p[0] for p in pairs], I32)
    kblock = jnp.asarray([p[1] for p in pairs], I32)

    def bias_spec(n):
        return pl.BlockSpec((1, 1, 1, tk, tqs), lambda bb, s, qi, kj: (bb, qi[s] * n_bias + n, kj[s], 0, 0))

    grid_spec = pltpu.PrefetchScalarGridSpec(
        num_scalar_prefetch=2,
        grid=(b, len(pairs)),
        in_specs=[pl.BlockSpec((1, width, tq), lambda bb, s, qi, kj: (bb, 0, qi[s])),
                  pl.BlockSpec((1, N_HEADS, tk, HEAD_DIM), lambda bb, s, qi, kj: (bb, 0, kj[s], 0)),
                  pl.BlockSpec((1, width, tk), lambda bb, s, qi, kj: (bb, 0, kj[s]))]
                 + [bias_spec(n) for n in range(n_bias)],
        out_specs=pl.BlockSpec((1, tq, width), lambda bb, s, qi, kj: (bb, qi[s], 0)),
        scratch_shapes=[pltpu.VMEM((N_HEADS, tq), F32), pltpu.VMEM((N_HEADS, tq), F32),
                        pltpu.VMEM((width, tq), F32), pltpu.VMEM((2, tk, tq), F32),
                        pltpu.VMEM((tk, tq), F32)])
    return pl.pallas_call(
        functools.partial(_attn_kernel, tq=tq, tk=tk, nkb=nkb, causal=causal, n_bias=n_bias),
        grid_spec=grid_spec,
        out_shape=jax.ShapeDtypeStruct((b, t, width), F32),
        compiler_params=_cparams(("arbitrary", "arbitrary")),
        name="masked_attention",
    )(qtile, kblock, qt, k_hm, vt, *([bias] * n_bias))


def _merge_kernel(x_ref, ys_ref, att_ref, za_ref, gs_ref, ga_ref, gate_ref, wps_ref, wpa_ref, wo_ref,
                  gfin_ref, o_ref, *, final_norm):
    ya = att_ref[0] * _silu(za_ref[0])
    merged = (jax.nn.sigmoid(gs_ref[0]) * _mxu_dot(ys_ref[...], wps_ref[...])
              + jax.nn.sigmoid(ga_ref[0]) * _mxu_dot(ya, wpa_ref[...]))
    x_new = x_ref[0] + gate_ref[0] * _mxu_dot(merged, wo_ref[...])
    if final_norm:
        x_new = x_new * lax.rsqrt(jnp.mean(x_new * x_new, axis=-1, keepdims=True) + EPS) * gfin_ref[...]
    o_ref[0] = x_new


def _merge(x, ys, att, za, gs, ga, gate, w_ps, w_pa, w_o, g_final, tm, final_norm):
    b, t, d = x.shape
    half = d // 2
    row = lambda width: pl.BlockSpec((1, tm, width), lambda i, j: (i, j, 0))
    const = lambda shape: pl.BlockSpec(shape, lambda i, j: (0,) * len(shape))
    return pl.pallas_call(
        functools.partial(_merge_kernel, final_norm=final_norm),
        grid=(b, t // tm),
        in_specs=[row(d), pl.BlockSpec((tm, half), lambda i, j: (j, i)), row(half), row(half), row(d), row(d),
                  pl.BlockSpec((1, 1, d), lambda i, j: (i, 0, 0)),
                  const((half, d)), const((half, d)), const((d, d)), const((1, d))],
        out_specs=row(d),
        out_shape=jax.ShapeDtypeStruct((b, t, d), F32),
        compiler_params=_cparams(("arbitrary", "arbitrary")),
        name="merge_out",
    )(x, ys, att, za, gs, ga, gate, w_ps.astype(MXU_DTYPE), w_pa.astype(MXU_DTYPE),
      w_o.astype(MXU_DTYPE), g_final.reshape(1, d))


def _row_tile(t, cap):
    tile = min(t, cap)
    assert t % tile == 0, (t, tile)
    return tile


def _key_blocks(s_len):
    s_pad = -(-s_len // LANES) * LANES
    for tk in (512, 384, 256, 128):
        if s_pad % tk == 0:
            return s_pad, tk
    raise AssertionError(s_len)


def _pad_keys(a, s_pad, axis):
    pad = s_pad - a.shape[axis]
    if pad == 0:
        return a
    widths = [(0, 0)] * a.ndim
    widths[axis] = (0, pad)
    return jnp.pad(a, widths)


def _layer(x, mod, lw, ssm_par, h0_re, h0_im, past, final_norm):
    b, t, d = x.shape
    shift, scale, gate = (m.reshape(b, 1, d) for m in jnp.split(mod, 3, axis=-1))
    tm = _row_tile(t, 256)
    (u, zs, za, gs, ga, k, v, ki, wit, qt, qit, vt, k_hm) = _inproj(
        x, shift, scale, lw["g_norm"], lw["wmain"], lw["wsmall"], tm)

    wb, wc, coef = ssm_par
    ys, hr, hi = _ssm(u, zs, b, h0_re, h0_im, wb, wc, coef,
                      lw["d_skip"], lw["w_glu"], lw["b_glu"], min(t * b, 256))

    ki_bf = ki.astype(MXU_DTYPE)
    if past is None:
        causal, n_valid = True, t
    else:
        past_k, past_v, past_ki = past
        causal, n_valid = False, past_k.shape[1] + t
        k_hm = jnp.concatenate([past_k.transpose(0, 2, 1, 3).astype(MXU_DTYPE), k_hm], axis=2)
        vt = jnp.concatenate([past_v.reshape(b, -1, d // 2).transpose(0, 2, 1).astype(MXU_DTYPE), vt], axis=2)
        ki_bf = jnp.concatenate([past_ki.astype(MXU_DTYPE), ki_bf], axis=1)
    n_sel = min(TOPK_MAX, n_valid // 4)
    s_pad, tk = _key_blocks(n_valid)
    k_hm = _pad_keys(k_hm, s_pad, 2)
    vt = _pad_keys(vt, s_pad, 2)
    ki_blk = _pad_keys(ki_bf, s_pad, 1).reshape(b, s_pad // tk, tk, IDX_DIM)

    tq = _row_tile(t, 256)
    bias = _select(qit, wit, ki_blk, tq, causal, n_valid, n_sel)
    att = _attention(qt, k_hm, vt, bias, tq, tk, causal)

    x_new = _merge(x, ys, att, za, gs, ga, gate, lw["w_ps"], lw["w_pa"], lw["w_o"], lw["g_final"],
                   _row_tile(t, 256), final_norm)
    gshape = (b, -1, STATE)
    return x_new, (k.reshape(b, t, N_HEADS, HEAD_DIM), v.reshape(b, t, N_HEADS, HEAD_DIM), ki,
                   hr.reshape(gshape), hi.reshape(gshape))


def kernel(x_prompt, x_sample, cache_k, cache_v, cache_kidx, state_ssm_re, state_ssm_im, c_prompt, c_sample, w_mod, b_mod, g_norm, w_in, a_re, a_im, log_dt, b_re, b_im, c_re, c_im, d_skip, w_glu, b_glu, w_ps, w_pa, w_o, g_final):
    depth = w_mod.shape[0]
    bp, _, d = x_prompt.shape
    g, p = a_re.shape[1:]
    mod = _modulation(jnp.concatenate([c_prompt, c_sample], axis=0), w_mod, b_mod)
    zeros = jnp.zeros((bp, g * p), F32)
    xp, xs = x_prompt, x_sample
    outs_p, outs_s = [], []
    for l in range(depth):
        wmain, wsmall = _inproj_weights(w_in[l], d)
        lw = dict(g_norm=g_norm[l], wmain=wmain, wsmall=wsmall, d_skip=d_skip[l], w_glu=w_glu[l],
                  b_glu=b_glu[l], w_ps=w_ps[l], w_pa=w_pa[l], w_o=w_o[l], g_final=g_final)
        ssm_par = _ssm_prep(a_re[l], a_im[l], log_dt[l], b_re[l], b_im[l], c_re[l], c_im[l])
        last = l == depth - 1
        xp, op = _layer(xp, mod[l, :bp], lw, ssm_par, zeros, zeros, None, last)
        xs, os_ = _layer(xs, mod[l, bp:], lw, ssm_par, state_ssm_re[l].reshape(-1, g * p),
                         state_ssm_im[l].reshape(-1, g * p), (cache_k[l], cache_v[l], cache_kidx[l]), last)
        outs_p.append(op)
        outs_s.append(os_)
    stack = lambda outs: tuple(jnp.stack(z) for z in zip(*outs))
    return (xp, xs) + stack(outs_p) + stack(outs_s)
```

```python
import functools
import math

import jax
import jax.numpy as jnp
from jax import lax
from jax.experimental import pallas as pl
from jax.experimental.pallas import tpu as pltpu

F32 = jnp.float32
I32 = jnp.int32
MXU_DTYPE = jnp.bfloat16

CHUNK = 64
SSM_GROUP = 16
STATE = 64
N_HEADS = 8
HEAD_DIM = 64
IDX_HEADS = 8
IDX_DIM = 64
TOPK_MAX = 256
EPS = 1e-6
LOG2E = math.log2(math.e)

LANES = 128
SUBLANES = 8
VREGS = 64
ROW_TILE = 256
V7X_VMEM_BYTES = 64 * 1024 * 1024
VMEM_LIMIT = V7X_VMEM_BYTES - 8 * 1024 * 1024

INT_MIN = -(2 ** 31)
INT16_MIN = -(2 ** 15)
MASK_NEG = -1e30
M_INIT = -1e29
NO_TIE_LIMIT = 2 ** 30
COUNT_ROWS = 32


def _cparams(sem):
    return pltpu.CompilerParams(dimension_semantics=sem, vmem_limit_bytes=VMEM_LIMIT)


def _silu(x):
    return x * jax.nn.sigmoid(x)


def _mxu_dot(a, b):
    return jnp.dot(a.astype(MXU_DTYPE), b.astype(MXU_DTYPE), preferred_element_type=F32)


def _split_hi_lo(a):
    hi = a.astype(jnp.bfloat16)
    lo = (a - hi.astype(F32)).astype(jnp.bfloat16)
    return hi, lo


def _dot_3pass(a, b):
    ah, al = _split_hi_lo(a)
    bh, bl = _split_hi_lo(b)
    d = functools.partial(jnp.dot, preferred_element_type=F32)
    return d(ah, bh) + (d(ah, bl) + d(al, bh))


def _mod_kernel(c_ref, w_ref, b_ref, o_ref):
    o_ref[...] = _dot_3pass(_silu(c_ref[...]), w_ref[...]) + b_ref[...]


def _modulation(c_all, w_mod, b_mod):
    depth, d, d3 = w_mod.shape
    rows = c_all.shape[0]
    cols = d3 // 4
    return pl.pallas_call(
        _mod_kernel,
        grid=(depth, d3 // cols),
        in_specs=[
            pl.BlockSpec((rows, d), lambda l, j: (0, 0)),
            pl.BlockSpec((None, d, cols), lambda l, j: (l, 0, j)),
            pl.BlockSpec((None, 1, cols), lambda l, j: (l, 0, j)),
        ],
        out_specs=pl.BlockSpec((None, rows, cols), lambda l, j: (l, 0, j)),
        out_shape=jax.ShapeDtypeStruct((depth, rows, d3), F32),
        compiler_params=_cparams(("arbitrary", "arbitrary")),
        name="adaln_mod",
    )(c_all, w_mod, b_mod.reshape(depth, 1, d3))


def _ssm_prep_kernel(are_ref, aim_ref, ldt_ref, bre_ref, bim_ref, cre_ref, cim_ref,
                     wb_ref, wc_ref, coef_ref):
    a_re = are_ref[...]
    a_im = aim_ref[...]
    dt = jnp.exp(ldt_ref[...])
    gp = a_re.shape[-1]

    mag = jnp.exp(a_re * dt)
    ang = a_im * dt
    ab_re, ab_im = mag * jnp.cos(ang), mag * jnp.sin(ang)
    den = a_re * a_re + a_im * a_im
    nr = ab_re - 1.0
    ni = ab_im
    f_re = (nr * a_re + ni * a_im) / den
    f_im = (ni * a_re - nr * a_im) / den

    rows_b = lax.broadcasted_iota(I32, bre_ref.shape, 0) // SSM_GROUP
    cols_b = lax.broadcasted_iota(I32, bre_ref.shape, 1) // STATE
    on_b = rows_b == cols_b
    b_re = bre_ref[...]
    b_im = bim_ref[...]
    wb_ref[:, 0:gp] = jnp.where(on_b, f_re * b_re - f_im * b_im, 0.0).astype(wb_ref.dtype)
    wb_ref[:, gp:2 * gp] = jnp.where(on_b, f_re * b_im + f_im * b_re, 0.0).astype(wb_ref.dtype)

    rows_c = lax.broadcasted_iota(I32, cre_ref.shape, 0) // STATE
    cols_c = lax.broadcasted_iota(I32, cre_ref.shape, 1) // SSM_GROUP
    on_c = rows_c == cols_c
    wc_ref[0:gp, :] = jnp.where(on_c, cre_ref[...], 0.0).astype(wc_ref.dtype)
    wc_ref[gp:2 * gp, :] = jnp.where(on_c, -cim_ref[...], 0.0).astype(wc_ref.dtype)

    coef_ref[0] = jnp.broadcast_to(ab_re, (SUBLANES, gp))
    coef_ref[1] = jnp.broadcast_to(ab_im, (SUBLANES, gp))


def _ssm_prep(a_re, a_im, log_dt, b_re, b_im, c_re, c_im):
    g, p = a_re.shape
    n = b_re.shape[-1]
    gp, gn = g * p, g * n
    row = lambda a: a.reshape(1, gp)
    ldt = jnp.broadcast_to(log_dt[:, None], (g, p)).reshape(1, gp)
    b_exp = lambda b: jnp.broadcast_to(b.transpose(2, 0, 1).reshape(1, n, gp), (g, n, gp)).reshape(gn, gp)
    c_exp = lambda c: jnp.broadcast_to(c.transpose(0, 2, 1).reshape(gp, 1, n), (gp, g, n)).reshape(gp, gn)
    return pl.pallas_call(
        _ssm_prep_kernel,
        out_shape=(jax.ShapeDtypeStruct((gn, 2 * gp), MXU_DTYPE),
                   jax.ShapeDtypeStruct((2 * gp, gn), MXU_DTYPE),
                   jax.ShapeDtypeStruct((2, SUBLANES, gp), F32)),
        compiler_params=pltpu.CompilerParams(vmem_limit_bytes=VMEM_LIMIT),
        name="ssm_prep",
    )(row(a_re), row(a_im), ldt, b_exp(b_re), b_exp(b_im), c_exp(c_re), c_exp(c_im))


_MAIN_SEGS = ("u", "zs", "za", "gs", "ga", "k", "v", "q", "qi")


def _main_offsets(d):
    half = d // 2
    widths = dict(u=half, zs=half, za=half, gs=d, ga=d, k=half, v=half, q=half, qi=half)
    offs, o = {}, 0
    for name in _MAIN_SEGS:
        offs[name] = (o, o + widths[name])
        o += widths[name]
    return offs, o


def _inproj_kernel(x_ref, shift_ref, scale_ref, g_ref, wmain_ref, wsmall_ref,
                   u_ref, zs_ref, za_ref, gs_ref, ga_ref, k_ref, v_ref, ki_ref,
                   wit_ref, qt_ref, qit_ref, vt_ref, khm_ref, *, offs):
    x = x_ref[0]
    h = x * lax.rsqrt(jnp.mean(x * x, axis=-1, keepdims=True) + EPS) * g_ref[...]
    h = h * (1.0 + scale_ref[0]) + shift_ref[0]
    hb = h.astype(MXU_DTYPE)

    def proj(name):
        c0, c1 = offs[name]
        return jnp.dot(hb, wmain_ref[:, c0:c1], preferred_element_type=F32)

    u_ref[...] = proj("u")
    zs_ref[...] = proj("zs")
    za_ref[0] = proj("za")
    gs_ref[0] = proj("gs")
    ga_ref[0] = proj("ga")

    zk = proj("k")
    k_ref[0] = zk
    for hh in range(N_HEADS):
        khm_ref[0, hh] = zk[:, hh * HEAD_DIM:(hh + 1) * HEAD_DIM].astype(khm_ref.dtype)
    zv = proj("v")
    v_ref[0] = zv
    vt_ref[0] = zv.T.astype(vt_ref.dtype)
    qt_ref[0] = proj("q").T.astype(qt_ref.dtype)
    qit_ref[0] = proj("qi").T.astype(qit_ref.dtype)

    zsm = jnp.dot(hb, wsmall_ref[...], preferred_element_type=F32)
    ki_ref[0] = zsm[:, 0:IDX_DIM]
    wit_ref[0] = zsm.T[IDX_DIM:IDX_DIM + IDX_HEADS, :] * (IDX_HEADS ** -0.5)


def _inproj_weights(w_in, d):
    half = d // 2
    idx_w = IDX_HEADS * IDX_DIM
    o = 0
    seg = {}
    for name, width in (("u", half), ("zs", half), ("q", half), ("k", half), ("v", half), ("za", half),
                        ("qi", idx_w), ("ki", IDX_DIM), ("wi", IDX_HEADS), ("gs", d), ("ga", d)):
        seg[name] = w_in[:, o:o + width]
        o += width
    seg["q"] = seg["q"] * (HEAD_DIM ** -0.5 * LOG2E)
    seg["qi"] = seg["qi"] * (IDX_DIM ** -0.5)
    wmain = jnp.concatenate([seg[n] for n in _MAIN_SEGS], axis=1).astype(MXU_DTYPE)
    wsmall = jnp.concatenate(
        [seg["ki"], seg["wi"], jnp.zeros((d, LANES - IDX_DIM - IDX_HEADS), w_in.dtype)], axis=1).astype(MXU_DTYPE)
    return wmain, wsmall


def _inproj(x, shift, scale, g_norm, wmain, wsmall, tm):
    b, t, d = x.shape
    half = d // 2
    offs, wcols = _main_offsets(d)
    nt = t // tm
    row = lambda width: pl.BlockSpec((1, tm, width), lambda i, j: (i, j, 0))
    hm = pl.BlockSpec((1, N_HEADS, tm, HEAD_DIM), lambda i, j: (i, 0, j, 0))
    tr = lambda rows: pl.BlockSpec((1, rows, tm), lambda i, j: (i, 0, j))
    tb = pl.BlockSpec((tm, half), lambda i, j: (j, i))
    per_b = pl.BlockSpec((1, 1, d), lambda i, j: (i, 0, 0))
    const = lambda shape: pl.BlockSpec(shape, lambda i, j: (0,) * len(shape))
    f = lambda *s: jax.ShapeDtypeStruct(s, F32)
    bf = lambda *s: jax.ShapeDtypeStruct(s, MXU_DTYPE)
    return pl.pallas_call(
        functools.partial(_inproj_kernel, offs=offs),
        grid=(b, nt),
        in_specs=[row(d), per_b, per_b, const((1, d)), const((d, wcols)), const((d, LANES))],
        out_specs=[tb, tb, row(half), row(d), row(d), row(half), row(half),
                   row(IDX_DIM), tr(IDX_HEADS), tr(half), tr(half), tr(half), hm],
        out_shape=[f(t, b * half), f(t, b * half), f(b, t, half), f(b, t, d), f(b, t, d),
                   f(b, t, half), f(b, t, half), f(b, t, IDX_DIM), f(b, IDX_HEADS, t),
                   bf(b, half, t), bf(b, half, t), bf(b, half, t), bf(b, N_HEADS, t, HEAD_DIM)],
        compiler_params=_cparams(("arbitrary", "arbitrary")),
        name="inproj",
    )(x, shift, scale, g_norm.reshape(1, d), wmain, wsmall)


def _ssm_kernel(u_ref, zs_ref, h0r_ref, h0i_ref, wb_ref, wc_ref, coef_ref, d_ref, wglu_ref, bglu_ref,
                y_ref, hr_ref, hi_ref, bu_s, u_s, y_s, *, tt, nb, gp):
    @pl.when(pl.program_id(0) == 0)
    def _():
        hr_ref[...] = h0r_ref[...]
        hi_ref[...] = h0i_ref[...]

    slabs = u_s.shape[0]
    dn = slabs * LANES
    for bb in range(nb):
        for j in range(slabs):
            lanes = slice(bb * dn + j * LANES, bb * dn + (j + 1) * LANES)
            u_s[j, pl.ds(bb, tt, stride=nb), :] = u_ref[:, lanes]
    u = jnp.concatenate([u_s[j] for j in range(slabs)], axis=-1)
    ub = u.astype(MXU_DTYPE)
    sw = gp // slabs
    for j in range(slabs):
        ch = slice(j * LANES, (j + 1) * LANES)
        for part in range(2):
            st = slice(part * gp + j * sw, part * gp + (j + 1) * sw)
            bu_s[:, st] = jnp.dot(ub[:, ch], wb_ref[ch, st], preferred_element_type=F32)

    groups = nb // SUBLANES

    def step(t, carry):
        a_re = coef_ref[0]
        a_im = coef_ref[1]
        out = []
        for g in range(groups):
            h_re, h_im = carry[2 * g], carry[2 * g + 1]
            rows = pl.ds(pl.multiple_of(t * nb + g * SUBLANES, SUBLANES), SUBLANES)
            n_re = (a_re * h_re - a_im * h_im) + bu_s[rows, 0:gp]
            n_im = (a_re * h_im + a_im * h_re) + bu_s[rows, gp:2 * gp]
            bu_s[rows, 0:gp] = n_re
            bu_s[rows, gp:2 * gp] = n_im
            out += [n_re, n_im]
        return tuple(out)

    init = []
    for g in range(groups):
        rows = slice(g * SUBLANES, (g + 1) * SUBLANES)
        init += [hr_ref[rows, :], hi_ref[rows, :]]
    fin = lax.fori_loop(0, tt, step, tuple(init), unroll=2)
    for g in range(groups):
        rows = slice(g * SUBLANES, (g + 1) * SUBLANES)
        hr_ref[rows, :] = fin[2 * g]
        hi_ref[rows, :] = fin[2 * g + 1]

    ys = []
    for j in range(slabs):
        ch = slice(j * LANES, (j + 1) * LANES)
        acc = d_ref[:, ch] * u[:, ch]
        for part in range(2):
            st = slice(part * gp + j * sw, part * gp + (j + 1) * sw)
            acc = acc + jnp.dot(bu_s[:, st].astype(MXU_DTYPE), wc_ref[st, ch], preferred_element_type=F32)
        ys.append(acc)
    y = jax.nn.gelu(jnp.concatenate(ys, axis=-1))
    y = y * jax.nn.sigmoid(_mxu_dot(y, wglu_ref[...]) + bglu_ref[...])
    for j in range(slabs):
        y_s[j] = y[:, j * LANES:(j + 1) * LANES]
    for bb in range(nb):
        for j in range(slabs):
            lanes = slice(bb * dn + j * LANES, bb * dn + (j + 1) * LANES)
            y_ref[:, lanes] = y_s[j, pl.ds(bb, tt, stride=nb), :] * _silu(zs_ref[:, lanes])


def _ssm(u, zs, nb, h0_re, h0_im, wb, wc, coef, d_skip, w_glu, b_glu, rows_per_step):
    t, width = u.shape
    dn = width // nb
    gp = wb.shape[1] // 2
    assert nb % SUBLANES == 0 and rows_per_step % nb == 0
    tt = rows_per_step // nb
    assert t % tt == 0
    row = pl.BlockSpec((tt, width), lambda j: (j, 0))
    const = lambda shape: pl.BlockSpec(shape, lambda j: (0,) * len(shape))
    return pl.pallas_call(
        functools.partial(_ssm_kernel, tt=tt, nb=nb, gp=gp),
        grid=(t // tt,),
        in_specs=[row, row, const((nb, gp)), const((nb, gp)), const(wb.shape), const(wc.shape),
                  const(coef.shape), const((1, dn)), const(w_glu.shape), const((1, dn))],
        out_specs=[row, const((nb, gp)), const((nb, gp))],
        out_shape=[jax.ShapeDtypeStruct((t, width), F32),
                   jax.ShapeDtypeStruct((nb, gp), F32),
                   jax.ShapeDtypeStruct((nb, gp), F32)],
        scratch_shapes=[pltpu.VMEM((rows_per_step, 2 * gp), F32),
                        pltpu.VMEM((dn // LANES, rows_per_step, LANES), F32),
                        pltpu.VMEM((dn // LANES, rows_per_step, LANES), F32)],
        compiler_params=_cparams(("arbitrary",)),
        name="ssm_scan",
    )(u, zs, h0_re, h0_im, wb, wc, coef,
      d_skip.reshape(1, dn), w_glu.astype(MXU_DTYPE), b_glu.reshape(1, dn))


def _num_key_blocks(i, tq, tk, nkb, causal):
    if not causal:
        return nkb
    return ((i + 1) * tq + (tk - 1)) // tk


def _key_chunk(tq, tk):
    lane_tiles = -(-tq // LANES)
    for sub in (256, 128, 64, 32, 16, 8):
        if tk % sub == 0 and (sub // SUBLANES) * lane_tiles <= VREGS // 2:
            return sub
    raise AssertionError((tq, tk))


def _select_kernel(qit_ref, wit_ref, ki_ref, bias_ref, keys_s, hi_s, lo_s, lg_s, *,
                   tq, tk, nkb, causal, n_valid, n_sel):
    i = pl.program_id(1)
    nb = _num_key_blocks(i, tq, tk, nkb, causal)
    if causal:
        q_t = i * tq + lax.broadcasted_iota(I32, (1, tq), 1)
        limit = (q_t // CHUNK + 1) * CHUNK
    else:
        limit = n_valid
    wit = wit_ref[0]

    def key_index(j):
        return j * tk + lax.broadcasted_iota(I32, (tk, tq), 0)

    sub = _key_chunk(tq, tk)

    def score_block(j, masked):
        for cc in range(tk // sub):
            rows = slice(cc * sub, (cc + 1) * sub)
            kb = ki_ref[0, j, rows, :]

            def logits(hh):
                return jnp.dot(kb, qit_ref[0, hh * IDX_DIM:(hh + 1) * IDX_DIM, :], preferred_element_type=F32)

            lg_s[0] = logits(0)
            s = jnp.zeros((sub, tq), F32)
            for hh in range(IDX_HEADS):
                if hh + 1 < IDX_HEADS:
                    lg_s[(hh + 1) % 2] = logits(hh + 1)
                s = s + wit[hh:hh + 1, :] * jnp.maximum(lg_s[hh % 2], 0.0)
            bits = pltpu.bitcast(s, I32)
            sign = bits >> 31
            key = (bits ^ (sign & 0x7FFFFFFF)) - sign
            if masked:
                k_idx = j * tk + cc * sub + lax.broadcasted_iota(I32, (sub, tq), 0)
                key = jnp.where(k_idx < limit, key, INT_MIN)
            keys_s[j, rows, :] = key
            hi_s[j, rows, :] = (key >> 16).astype(jnp.int16)

    n_full = ((i * tq) // CHUNK + 1) * CHUNK // tk if causal else n_valid // tk

    def full_block(j, c):
        score_block(j, False)
        return c

    def edge_block(j, c):
        score_block(j, True)
        return c

    lax.fori_loop(0, n_full, full_block, 0)
    lax.fori_loop(n_full, nb, edge_block, 0)

    def count(pred_fns):
        def body(j, accs):
            key = keys_s[j]
            return tuple(acc + jnp.sum(jnp.where(fn(key), 1.0, 0.0).reshape(tk // COUNT_ROWS, COUNT_ROWS, tq),
                                       axis=0)
                         for fn, acc in zip(pred_fns, accs))
        accs = lax.fori_loop(0, nb, body, tuple(jnp.zeros((COUNT_ROWS, tq), F32) for _ in pred_fns))
        return [jnp.sum(a, axis=0, keepdims=True) for a in accs]

    def count16(ref, thr16, strict):
        def body(j, acc):
            blk = ref[j]
            hit = blk > thr16 if strict else blk >= thr16
            ind = jnp.where(hit, jnp.int16(1), jnp.int16(0))
            part = ind[0:COUNT_ROWS]
            for r in range(1, tk // COUNT_ROWS):
                part = part + ind[r * COUNT_ROWS:(r + 1) * COUNT_ROWS]
            return acc + part.astype(F32)
        acc = lax.fori_loop(0, nb, body, jnp.zeros((COUNT_ROWS, tq), F32))
        return jnp.sum(acc, axis=0, keepdims=True)

    def radix16(ref, need, count_all):
        def bit_step(bi, carry):
            u, c_acc = carry
            cand = u | lax.shift_left(jnp.int32(1), 15 - bi)
            cnt = count16(ref, (cand + INT16_MIN).astype(jnp.int16), False)
            keep = cnt >= need
            return jnp.where(keep, cand, u), jnp.where(keep, cnt, c_acc)
        u, c_acc = lax.fori_loop(0, 16, bit_step, (jnp.zeros((1, tq), I32), count_all))
        return u + INT16_MIN, c_acc

    n_adm = jnp.broadcast_to(jnp.asarray(limit, F32), (1, tq))
    n_all = jnp.broadcast_to((nb * tk).astype(F32) if causal else jnp.float32(nb * tk), (1, tq))
    hi_sel, c_ge_hi = radix16(hi_s, n_sel, n_all)
    c_gt_hi = count16(hi_s, hi_sel.astype(jnp.int16), True)
    want_lo = n_sel - c_gt_hi

    def low_block(j, c):
        key = keys_s[j]
        low = (key & 0xFFFF) + INT16_MIN
        lo_s[j] = jnp.where((key >> 16) == hi_sel, low, INT16_MIN).astype(jnp.int16)
        return c

    lax.fori_loop(0, nb, low_block, 0)
    lo_sel, c_ge_lo = radix16(lo_s, want_lo, c_ge_hi - c_gt_hi)
    kth = lax.shift_left(hi_sel, 16) | (lo_sel - INT16_MIN)
    c_ge = jnp.where(kth == INT_MIN, n_adm, c_gt_hi + c_ge_lo)
    thr = jnp.maximum(kth, INT_MIN + 1)
    tie_q = c_ge > n_sel
    has_tie = jnp.max(jnp.where(tie_q, 1.0, 0.0)) > 0.0

    @pl.when(jnp.logical_not(has_tie))
    def _():
        def bias_block(j, c):
            bias_ref[0, 0, j] = jnp.where(keys_s[j] >= thr, 0.0, MASK_NEG).astype(bias_ref.dtype)
            return c

        lax.fori_loop(0, nb, bias_block, 0)

    @pl.when(has_tie)
    def _():
        (c_gt,) = count([lambda kk: kk > thr])
        want = n_sel - c_gt
        tri = jnp.where(lax.broadcasted_iota(I32, (tk, tk), 0) >= lax.broadcasted_iota(I32, (tk, tk), 1),
                        1.0, 0.0).astype(jnp.bfloat16)

        def body(j, carry):
            before, below = carry
            eq = jnp.where(keys_s[j] == thr, 1.0, 0.0).astype(jnp.bfloat16)
            pref = before + jnp.dot(tri, eq, preferred_element_type=F32)
            below = below + jnp.sum(jnp.where(pref < want, 1.0, 0.0), axis=0, keepdims=True)
            return pref[tk - 1:tk, :], below

        _, below = lax.fori_loop(0, nb, body, (jnp.zeros((1, tq), F32), jnp.zeros((1, tq), F32)))
        jstar = jnp.where(tie_q, below.astype(I32), NO_TIE_LIMIT)

        def bias_block(j, c):
            key = keys_s[j]
            tie_ok = jnp.where(key_index(j) <= jstar, 0.0, MASK_NEG)
            bias = jnp.where(key > thr, 0.0, jnp.where(key == thr, tie_ok, MASK_NEG))
            bias_ref[0, 0, j] = bias.astype(bias_ref.dtype)
            return c

        lax.fori_loop(0, nb, bias_block, 0)

    def masked_block(j, c):
        bias_ref[0, 0, j] = jnp.full((tk, tq), MASK_NEG, bias_ref.dtype)
        return c

    lax.fori_loop(nb, nkb, masked_block, 0)


def _select(qit, wit, ki_blk, tq, causal, n_valid, n_sel):
    b, width, t = qit.shape
    _, nkb, tk, _ = ki_blk.shape
    nq = t // tq
    return pl.pallas_call(
        functools.partial(_select_kernel, tq=tq, tk=tk, nkb=nkb, causal=causal, n_valid=n_valid, n_sel=n_sel),
        grid=(b, nq),
        in_specs=[pl.BlockSpec((1, width, tq), lambda i, j: (i, 0, j)),
                  pl.BlockSpec((1, IDX_HEADS, tq), lambda i, j: (i, 0, j)),
                  pl.BlockSpec((1, nkb, tk, IDX_DIM), lambda i, j: (i, 0, 0, 0))],
        out_specs=pl.BlockSpec((1, 1, nkb, tk, tq), lambda i, j: (i, j, 0, 0, 0)),
        out_shape=jax.ShapeDtypeStruct((b, nq, nkb, tk, tq), jnp.bfloat16),
        scratch_shapes=[pltpu.VMEM((nkb, tk, tq), I32), pltpu.VMEM((nkb, tk, tq), jnp.int16),
                        pltpu.VMEM((nkb, tk, tq), jnp.int16), pltpu.VMEM((2, _key_chunk(tq, tk), tq), F32)],
        compiler_params=_cparams(("arbitrary", "arbitrary")),
        name="topk_select",
    )(qit, wit, ki_blk)


def _attn_kernel(qtile_ref, kblock_ref, qt_ref, k_ref, vt_ref, bias_ref, o_ref, m_s, l_s, acc_s, s_s, bias_s, *,
                 tq, tk, nkb, causal):
    step = pl.program_id(1)
    j = kblock_ref[step]
    last = _num_key_blocks(qtile_ref[step], tq, tk, nkb, causal) - 1

    @pl.when(j == 0)
    def _():
        m_s[...] = jnp.full(m_s.shape, M_INIT, F32)
        l_s[...] = jnp.zeros(l_s.shape, F32)
        acc_s[...] = jnp.zeros(acc_s.shape, F32)

    sub = _key_chunk(tq, tk)
    chunks = [slice(c * sub, (c + 1) * sub) for c in range(tk // sub)]
    m_all = m_s[...]
    l_all = l_s[...]
    bias_s[...] = bias_ref[0, 0, 0].astype(F32)

    def stage_a(hh, keys, mx):
        s = jnp.dot(k_ref[0, hh, keys, :], qt_ref[0, hh * HEAD_DIM:(hh + 1) * HEAD_DIM, :],
                    preferred_element_type=F32) + bias_s[keys, :]
        s_s[hh % 2, keys, :] = s
        return jnp.maximum(mx, jnp.max(s, axis=0, keepdims=True))

    def stage_b(hh, keys, m_new, lsum, pv):
        p = jnp.exp2(s_s[hh % 2, keys, :] - m_new)
        lsum = lsum + jnp.sum(p, axis=0, keepdims=True)
        pv = pv + jnp.dot(vt_ref[0, hh * HEAD_DIM:(hh + 1) * HEAD_DIM, keys], p.astype(MXU_DTYPE),
                          preferred_element_type=F32)
        return lsum, pv

    m_new = m_all[0:1, :]
    for keys in chunks:
        m_new = stage_a(0, keys, m_new)
    m_out, l_out = [], []
    for hh in range(N_HEADS):
        rows = slice(hh * HEAD_DIM, (hh + 1) * HEAD_DIM)
        lsum = jnp.zeros((1, tq), F32)
        pv = jnp.zeros((HEAD_DIM, tq), F32)
        m_next = m_all[hh + 1:hh + 2, :] if hh + 1 < N_HEADS else None
        for keys in chunks:
            if m_next is not None:
                m_next = stage_a(hh + 1, keys, m_next)
            lsum, pv = stage_b(hh, keys, m_new, lsum, pv)
        alpha = jnp.exp2(m_all[hh:hh + 1, :] - m_new)
        l_out.append(alpha * l_all[hh:hh + 1, :] + lsum)
        acc_s[rows, :] = alpha * acc_s[rows, :] + pv
        m_out.append(m_new)
        m_new = m_next
    m_s[...] = jnp.concatenate(m_out, axis=0)
    l_s[...] = jnp.concatenate(l_out, axis=0)

    @pl.when(j == last)
    def _():
        out_t = jnp.concatenate(
            [acc_s[hh * HEAD_DIM:(hh + 1) * HEAD_DIM, :] / l_s[hh:hh + 1, :] for hh in range(N_HEADS)], axis=0)
        o_ref[0] = out_t.T


def _attention(qt, k_hm, vt, bias, tq, tk, causal):
    b, width, t = qt.shape
    s_len = vt.shape[-1]
    nq, nkb = t // tq, s_len // tk
    pairs = [(i, j) for i in range(nq) for j in range(_num_key_blocks(i, tq, tk, nkb, causal))]
    qtile = jnp.asarray([p[0] for p in pairs], I32)
    kblock = jnp.asarray([p[1] for p in pairs], I32)
    grid_spec = pltpu.PrefetchScalarGridSpec(
        num_scalar_prefetch=2,
        grid=(b, len(pairs)),
        in_specs=[pl.BlockSpec((1, width, tq), lambda bb, s, qi, kj: (bb, 0, qi[s])),
                  pl.BlockSpec((1, N_HEADS, tk, HEAD_DIM), lambda bb, s, qi, kj: (bb, 0, kj[s], 0)),
                  pl.BlockSpec((1, width, tk), lambda bb, s, qi, kj: (bb, 0, kj[s])),
                  pl.BlockSpec((1, 1, 1, tk, tq), lambda bb, s, qi, kj: (bb, qi[s], kj[s], 0, 0))],
        out_specs=pl.BlockSpec((1, tq, width), lambda bb, s, qi, kj: (bb, qi[s], 0)),
        scratch_shapes=[pltpu.VMEM((N_HEADS, tq), F32), pltpu.VMEM((N_HEADS, tq), F32),
                        pltpu.VMEM((width, tq), F32), pltpu.VMEM((2, tk, tq), F32),
                        pltpu.VMEM((tk, tq), F32)])
    return pl.pallas_call(
        functools.partial(_attn_kernel, tq=tq, tk=tk, nkb=nkb, causal=causal),
        grid_spec=grid_spec,
        out_shape=jax.ShapeDtypeStruct((b, t, width), F32),
        compiler_params=_cparams(("arbitrary", "arbitrary")),
        name="masked_attention",
    )(qtile, kblock, qt, k_hm, vt, bias)


def _merge_kernel(x_ref, ys_ref, att_ref, za_ref, gs_ref, ga_ref, gate_ref, wps_ref, wpa_ref, wo_ref,
                  gfin_ref, o_ref, *, final_norm):
    ya = att_ref[0] * _silu(za_ref[0])
    merged = (jax.nn.sigmoid(gs_ref[0]) * _mxu_dot(ys_ref[...], wps_ref[...])
              + jax.nn.sigmoid(ga_ref[0]) * _mxu_dot(ya, wpa_ref[...]))
    x_new = x_ref[0] + gate_ref[0] * _mxu_dot(merged, wo_ref[...])
    if final_norm:
        x_new = x_new * lax.rsqrt(jnp.mean(x_new * x_new, axis=-1, keepdims=True) + EPS) * gfin_ref[...]
    o_ref[0] = x_new


def _merge(x, ys, att, za, gs, ga, gate, w_ps, w_pa, w_o, g_final, tm, final_norm):
    b, t, d = x.shape
    half = d // 2
    row = lambda width: pl.BlockSpec((1, tm, width), lambda i, j: (i, j, 0))
    const = lambda shape: pl.BlockSpec(shape, lambda i, j: (0,) * len(shape))
    return pl.pallas_call(
        functools.partial(_merge_kernel, final_norm=final_norm),
        grid=(b, t // tm),
        in_specs=[row(d), pl.BlockSpec((tm, half), lambda i, j: (j, i)), row(half), row(half), row(d), row(d),
                  pl.BlockSpec((1, 1, d), lambda i, j: (i, 0, 0)),
                  const((half, d)), const((half, d)), const((d, d)), const((1, d))],
        out_specs=row(d),
        out_shape=jax.ShapeDtypeStruct((b, t, d), F32),
        compiler_params=_cparams(("arbitrary", "arbitrary")),
        name="merge_out",
    )(x, ys, att, za, gs, ga, gate, w_ps.astype(MXU_DTYPE), w_pa.astype(MXU_DTYPE),
      w_o.astype(MXU_DTYPE), g_final.reshape(1, d))


def _row_tile(t, cap):
    tile = min(t, cap)
    assert t % tile == 0, (t, tile)
    return tile


def _key_blocks(s_len):
    s_pad = -(-s_len // LANES) * LANES
    for tk in (512, 384, 256, 128):
        if s_pad % tk == 0:
            return s_pad, tk
    raise AssertionError(s_len)


def _layer(x, mod, lw, ssm_par, h0_re, h0_im, past, final_norm):
    b, t, d = x.shape
    shift, scale, gate = (m.reshape(b, 1, d) for m in jnp.split(mod, 3, axis=-1))
    tm = _row_tile(t, ROW_TILE)
    (u, zs, za, gs, ga, k, v, ki, wit, qt, qit, vt, k_hm) = _inproj(
        x, shift, scale, lw["g_norm"], lw["wmain"], lw["wsmall"], tm)

    wb, wc, coef = ssm_par
    ys, hr, hi = _ssm(u, zs, b, h0_re, h0_im, wb, wc, coef,
                      lw["d_skip"], lw["w_glu"], lw["b_glu"], min(t * b, ROW_TILE))

    ki_bf = ki.astype(MXU_DTYPE)
    if past is None:
        causal, n_valid = True, t
        s_pad, tk = _key_blocks(n_valid)
        assert s_pad == n_valid
    else:
        past_k_hm, past_vt, past_ki = past
        causal, n_valid = False, past_ki.shape[1] + t
        s_pad, tk = _key_blocks(n_valid)
        pad = s_pad - n_valid
        zeros = lambda shape: jnp.zeros(shape, MXU_DTYPE)
        k_hm = jnp.concatenate([past_k_hm, k_hm, zeros((b, N_HEADS, pad, HEAD_DIM))], axis=2)
        vt = jnp.concatenate([past_vt, vt, zeros((b, d // 2, pad))], axis=2)
        ki_bf = jnp.concatenate([past_ki, ki_bf, zeros((b, pad, IDX_DIM))], axis=1)
    n_sel = min(TOPK_MAX, n_valid // 4)
    ki_blk = ki_bf.reshape(b, s_pad // tk, tk, IDX_DIM)

    tq = _row_tile(t, ROW_TILE)
    bias = _select(qit, wit, ki_blk, tq, causal, n_valid, n_sel)
    att = _attention(qt, k_hm, vt, bias, tq, tk, causal)

    x_new = _merge(x, ys, att, za, gs, ga, gate, lw["w_ps"], lw["w_pa"], lw["w_o"], lw["g_final"],
                   _row_tile(t, ROW_TILE), final_norm)
    gshape = (b, -1, STATE)
    return x_new, (k.reshape(b, t, N_HEADS, HEAD_DIM), v.reshape(b, t, N_HEADS, HEAD_DIM), ki,
                   hr.reshape(gshape), hi.reshape(gshape))


def kernel(x_prompt, x_sample, cache_k, cache_v, cache_kidx, state_ssm_re, state_ssm_im, c_prompt, c_sample, w_mod, b_mod, g_norm, w_in, a_re, a_im, log_dt, b_re, b_im, c_re, c_im, d_skip, w_glu, b_glu, w_ps, w_pa, w_o, g_final):
    depth = w_mod.shape[0]
    bp, _, d = x_prompt.shape
    g, p = a_re.shape[1:]
    mod = _modulation(jnp.concatenate([c_prompt, c_sample], axis=0), w_mod, b_mod)
    zeros = jnp.zeros((bp, g * p), F32)
    bs, plen = cache_k.shape[1:3]
    past_k_hm = cache_k.transpose(0, 1, 3, 2, 4).astype(MXU_DTYPE)
    past_vt = cache_v.reshape(depth, bs, plen, d // 2).transpose(0, 1, 3, 2).astype(MXU_DTYPE)
    past_ki = cache_kidx.astype(MXU_DTYPE)
    xp, xs = x_prompt, x_sample
    outs_p, outs_s = [], []
    for l in range(depth):
        wmain, wsmall = _inproj_weights(w_in[l], d)
        lw = dict(g_norm=g_norm[l], wmain=wmain, wsmall=wsmall, d_skip=d_skip[l], w_glu=w_glu[l],
                  b_glu=b_glu[l], w_ps=w_ps[l], w_pa=w_pa[l], w_o=w_o[l], g_final=g_final)
        ssm_par = _ssm_prep(a_re[l], a_im[l], log_dt[l], b_re[l], b_im[l], c_re[l], c_im[l])
        last = l == depth - 1
        xp, op = _layer(xp, mod[l, :bp], lw, ssm_par, zeros, zeros, None, last)
        xs, os_ = _layer(xs, mod[l, bp:], lw, ssm_par, state_ssm_re[l].reshape(-1, g * p),
                         state_ssm_im[l].reshape(-1, g * p), (past_k_hm[l], past_vt[l], past_ki[l]), last)
        outs_p.append(op)
        outs_s.append(os_)
    stack = lambda outs: tuple(jnp.stack(z) for z in zip(*outs))
    return (xp, xs) + stack(outs_p) + stack(outs_s)
```

```python
import functools
import math

import jax
import jax.numpy as jnp
from jax import lax
from jax.experimental import pallas as pl
from jax.experimental.pallas import tpu as pltpu

F32 = jnp.float32
I32 = jnp.int32
MXU_DTYPE = jnp.bfloat16

CHUNK = 64
SSM_GROUP = 16
STATE = 64
N_HEADS = 8
HEAD_DIM = 64
IDX_HEADS = 8
IDX_DIM = 64
TOPK_MAX = 256
EPS = 1e-6
LOG2E = math.log2(math.e)

LANES = 128
SUBLANES = 8
VREGS = 64
ROW_TILE = 256
PROJ_ROW_TILE = 512
V7X_VMEM_BYTES = 64 * 1024 * 1024
VMEM_LIMIT = V7X_VMEM_BYTES - 8 * 1024 * 1024

INT_MIN = -(2 ** 31)
INT16_MIN = -(2 ** 15)
MASK_NEG = -1e30
M_INIT = -1e29
NO_TIE_LIMIT = 2 ** 30
COUNT_ROWS = 32


def _cparams(sem):
    return pltpu.CompilerParams(dimension_semantics=sem, vmem_limit_bytes=VMEM_LIMIT)


def _silu(x):
    return x * jax.nn.sigmoid(x)


def _mxu_dot(a, b):
    return jnp.dot(a.astype(MXU_DTYPE), b.astype(MXU_DTYPE), preferred_element_type=F32)


def _split_hi_lo(a):
    hi = a.astype(jnp.bfloat16)
    lo = (a - hi.astype(F32)).astype(jnp.bfloat16)
    return hi, lo


def _dot_3pass(a, b):
    ah, al = _split_hi_lo(a)
    bh, bl = _split_hi_lo(b)
    d = functools.partial(jnp.dot, preferred_element_type=F32)
    return d(ah, bh) + (d(ah, bl) + d(al, bh))


def _mod_kernel(c_ref, w_ref, b_ref, o_ref):
    o_ref[...] = _dot_3pass(_silu(c_ref[...]), w_ref[...]) + b_ref[...]


def _modulation(c_all, w_mod, b_mod):
    depth, d, d3 = w_mod.shape
    rows = c_all.shape[0]
    cols = d3 // 4
    return pl.pallas_call(
        _mod_kernel,
        grid=(depth, d3 // cols),
        in_specs=[
            pl.BlockSpec((rows, d), lambda l, j: (0, 0)),
            pl.BlockSpec((None, d, cols), lambda l, j: (l, 0, j)),
            pl.BlockSpec((None, 1, cols), lambda l, j: (l, 0, j)),
        ],
        out_specs=pl.BlockSpec((None, rows, cols), lambda l, j: (l, 0, j)),
        out_shape=jax.ShapeDtypeStruct((depth, rows, d3), F32),
        compiler_params=_cparams(("arbitrary", "arbitrary")),
        name="adaln_mod",
    )(c_all, w_mod, b_mod.reshape(depth, 1, d3))


def _ssm_prep_kernel(are_ref, aim_ref, ldt_ref, bre_ref, bim_ref, cre_ref, cim_ref,
                     wb_ref, wc_ref, coef_ref):
    a_re = are_ref[...]
    a_im = aim_ref[...]
    dt = jnp.exp(ldt_ref[...])
    gp = a_re.shape[-1]

    mag = jnp.exp(a_re * dt)
    ang = a_im * dt
    ab_re, ab_im = mag * jnp.cos(ang), mag * jnp.sin(ang)
    den = a_re * a_re + a_im * a_im
    nr = ab_re - 1.0
    ni = ab_im
    f_re = (nr * a_re + ni * a_im) / den
    f_im = (ni * a_re - nr * a_im) / den

    rows_b = lax.broadcasted_iota(I32, bre_ref.shape, 0) // SSM_GROUP
    cols_b = lax.broadcasted_iota(I32, bre_ref.shape, 1) // STATE
    on_b = rows_b == cols_b
    b_re = bre_ref[...]
    b_im = bim_ref[...]
    wb_ref[:, 0:gp] = jnp.where(on_b, f_re * b_re - f_im * b_im, 0.0).astype(wb_ref.dtype)
    wb_ref[:, gp:2 * gp] = jnp.where(on_b, f_re * b_im + f_im * b_re, 0.0).astype(wb_ref.dtype)

    rows_c = lax.broadcasted_iota(I32, cre_ref.shape, 0) // STATE
    cols_c = lax.broadcasted_iota(I32, cre_ref.shape, 1) // SSM_GROUP
    on_c = rows_c == cols_c
    wc_ref[0:gp, :] = jnp.where(on_c, cre_ref[...], 0.0).astype(wc_ref.dtype)
    wc_ref[gp:2 * gp, :] = jnp.where(on_c, -cim_ref[...], 0.0).astype(wc_ref.dtype)

    coef_ref[0] = jnp.broadcast_to(ab_re, (SUBLANES, gp))
    coef_ref[1] = jnp.broadcast_to(ab_im, (SUBLANES, gp))


def _ssm_prep(a_re, a_im, log_dt, b_re, b_im, c_re, c_im):
    g, p = a_re.shape
    n = b_re.shape[-1]
    gp, gn = g * p, g * n
    row = lambda a: a.reshape(1, gp)
    ldt = jnp.broadcast_to(log_dt[:, None], (g, p)).reshape(1, gp)
    b_exp = lambda b: jnp.broadcast_to(b.transpose(2, 0, 1).reshape(1, n, gp), (g, n, gp)).reshape(gn, gp)
    c_exp = lambda c: jnp.broadcast_to(c.transpose(0, 2, 1).reshape(gp, 1, n), (gp, g, n)).reshape(gp, gn)
    return pl.pallas_call(
        _ssm_prep_kernel,
        out_shape=(jax.ShapeDtypeStruct((gn, 2 * gp), MXU_DTYPE),
                   jax.ShapeDtypeStruct((2 * gp, gn), MXU_DTYPE),
                   jax.ShapeDtypeStruct((2, SUBLANES, gp), F32)),
        compiler_params=pltpu.CompilerParams(vmem_limit_bytes=VMEM_LIMIT),
        name="ssm_prep",
    )(row(a_re), row(a_im), ldt, b_exp(b_re), b_exp(b_im), c_exp(c_re), c_exp(c_im))


_MAIN_SEGS = ("u", "zs", "za", "gs", "ga", "k", "v", "q", "qi")


def _main_offsets(d):
    half = d // 2
    widths = dict(u=half, zs=half, za=half, gs=d, ga=d, k=half, v=half, q=half, qi=half)
    offs, o = {}, 0
    for name in _MAIN_SEGS:
        offs[name] = (o, o + widths[name])
        o += widths[name]
    return offs, o


def _inproj_kernel(x_ref, shift_ref, scale_ref, g_ref, wmain_ref, wsmall_ref,
                   u_ref, zs_ref, za_ref, gs_ref, ga_ref, k_ref, v_ref, ki_ref,
                   wit_ref, qt_ref, qit_ref, vt_ref, khm_ref, *, offs):
    x = x_ref[0]
    h = x * lax.rsqrt(jnp.mean(x * x, axis=-1, keepdims=True) + EPS) * g_ref[...]
    h = h * (1.0 + scale_ref[0]) + shift_ref[0]
    hb = h.astype(MXU_DTYPE)

    def proj(name):
        c0, c1 = offs[name]
        return jnp.dot(hb, wmain_ref[:, c0:c1], preferred_element_type=F32)

    u_ref[...] = proj("u")
    zs_ref[...] = proj("zs")
    za_ref[0] = proj("za")
    gs_ref[0] = proj("gs")
    ga_ref[0] = proj("ga")

    zk = proj("k")
    k_ref[0] = zk
    for hh in range(N_HEADS):
        khm_ref[0, hh] = zk[:, hh * HEAD_DIM:(hh + 1) * HEAD_DIM].astype(khm_ref.dtype)
    zv = proj("v")
    v_ref[0] = zv
    vt_ref[0] = zv.T.astype(vt_ref.dtype)
    qt_ref[0] = proj("q").T.astype(qt_ref.dtype)
    qit_ref[0] = proj("qi").T.astype(qit_ref.dtype)

    zsm = jnp.dot(hb, wsmall_ref[...], preferred_element_type=F32)
    ki_ref[0] = zsm[:, 0:IDX_DIM]
    wit_ref[0] = zsm.T[IDX_DIM:IDX_DIM + IDX_HEADS, :] * (IDX_HEADS ** -0.5)


def _inproj_weights(w_in, d):
    half = d // 2
    idx_w = IDX_HEADS * IDX_DIM
    o = 0
    seg = {}
    for name, width in (("u", half), ("zs", half), ("q", half), ("k", half), ("v", half), ("za", half),
                        ("qi", idx_w), ("ki", IDX_DIM), ("wi", IDX_HEADS), ("gs", d), ("ga", d)):
        seg[name] = w_in[:, o:o + width]
        o += width
    seg["q"] = seg["q"] * (HEAD_DIM ** -0.5 * LOG2E)
    seg["qi"] = seg["qi"] * (IDX_DIM ** -0.5)
    wmain = jnp.concatenate([seg[n] for n in _MAIN_SEGS], axis=1).astype(MXU_DTYPE)
    wsmall = jnp.concatenate(
        [seg["ki"], seg["wi"], jnp.zeros((d, LANES - IDX_DIM - IDX_HEADS), w_in.dtype)], axis=1).astype(MXU_DTYPE)
    return wmain, wsmall


def _inproj(x, shift, scale, g_norm, wmain, wsmall, tm):
    b, t, d = x.shape
    half = d // 2
    offs, wcols = _main_offsets(d)
    nt = t // tm
    row = lambda width: pl.BlockSpec((1, tm, width), lambda i, j: (i, j, 0))
    hm = pl.BlockSpec((1, N_HEADS, tm, HEAD_DIM), lambda i, j: (i, 0, j, 0))
    tr = lambda rows: pl.BlockSpec((1, rows, tm), lambda i, j: (i, 0, j))
    tb = pl.BlockSpec((tm, half), lambda i, j: (j, i))
    per_b = pl.BlockSpec((1, 1, d), lambda i, j: (i, 0, 0))
    const = lambda shape: pl.BlockSpec(shape, lambda i, j: (0,) * len(shape), pipeline_mode=pl.Buffered(1))
    f = lambda *s: jax.ShapeDtypeStruct(s, F32)
    bf = lambda *s: jax.ShapeDtypeStruct(s, MXU_DTYPE)
    return pl.pallas_call(
        functools.partial(_inproj_kernel, offs=offs),
        grid=(b, nt),
        in_specs=[row(d), per_b, per_b, const((1, d)), const((d, wcols)), const((d, LANES))],
        out_specs=[tb, tb, row(half), row(d), row(d), row(half), row(half),
                   row(IDX_DIM), tr(IDX_HEADS), tr(half), tr(half), tr(half), hm],
        out_shape=[f(t, b * half), f(t, b * half), f(b, t, half), f(b, t, d), f(b, t, d),
                   f(b, t, half), f(b, t, half), f(b, t, IDX_DIM), f(b, IDX_HEADS, t),
                   bf(b, half, t), bf(b, half, t), bf(b, half, t), bf(b, N_HEADS, t, HEAD_DIM)],
        compiler_params=_cparams(("arbitrary", "arbitrary")),
        name="inproj",
    )(x, shift, scale, g_norm.reshape(1, d), wmain, wsmall)


def _ssm_kernel(u_ref, zs_ref, h0r_ref, h0i_ref, wb_ref, wc_ref, coef_ref, d_ref, wglu_ref, bglu_ref,
                y_ref, hr_ref, hi_ref, bu_s, u_s, y_s, *, tt, nb, gp):
    @pl.when(pl.program_id(0) == 0)
    def _():
        hr_ref[...] = h0r_ref[...]
        hi_ref[...] = h0i_ref[...]

    slabs = u_s.shape[0]
    dn = slabs * LANES
    for bb in range(nb):
        for j in range(slabs):
            lanes = slice(bb * dn + j * LANES, bb * dn + (j + 1) * LANES)
            u_s[j, pl.ds(bb, tt, stride=nb), :] = u_ref[:, lanes]
    u = jnp.concatenate([u_s[j] for j in range(slabs)], axis=-1)
    ub = u.astype(MXU_DTYPE)
    sw = gp // slabs
    for j in range(slabs):
        ch = slice(j * LANES, (j + 1) * LANES)
        for part in range(2):
            st = slice(part * gp + j * sw, part * gp + (j + 1) * sw)
            bu_s[:, st] = jnp.dot(ub[:, ch], wb_ref[ch, st], preferred_element_type=F32)

    groups = nb // SUBLANES

    def step(t, carry):
        a_re = coef_ref[0]
        a_im = coef_ref[1]
        out = []
        for g in range(groups):
            h_re, h_im = carry[2 * g], carry[2 * g + 1]
            rows = pl.ds(pl.multiple_of(t * nb + g * SUBLANES, SUBLANES), SUBLANES)
            n_re = (a_re * h_re - a_im * h_im) + bu_s[rows, 0:gp]
            n_im = (a_re * h_im + a_im * h_re) + bu_s[rows, gp:2 * gp]
            bu_s[rows, 0:gp] = n_re
            bu_s[rows, gp:2 * gp] = n_im
            out += [n_re, n_im]
        return tuple(out)

    init = []
    for g in range(groups):
        rows = slice(g * SUBLANES, (g + 1) * SUBLANES)
        init += [hr_ref[rows, :], hi_ref[rows, :]]
    fin = lax.fori_loop(0, tt, step, tuple(init), unroll=2)
    for g in range(groups):
        rows = slice(g * SUBLANES, (g + 1) * SUBLANES)
        hr_ref[rows, :] = fin[2 * g]
        hi_ref[rows, :] = fin[2 * g + 1]

    ys = []
    for j in range(slabs):
        ch = slice(j * LANES, (j + 1) * LANES)
        acc = d_ref[:, ch] * u[:, ch]
        for part in range(2):
            st = slice(part * gp + j * sw, part * gp + (j + 1) * sw)
            acc = acc + jnp.dot(bu_s[:, st].astype(MXU_DTYPE), wc_ref[st, ch], preferred_element_type=F32)
        ys.append(acc)
    y = jax.nn.gelu(jnp.concatenate(ys, axis=-1))
    y = y * jax.nn.sigmoid(_mxu_dot(y, wglu_ref[...]) + bglu_ref[...])
    for j in range(slabs):
        y_s[j] = y[:, j * LANES:(j + 1) * LANES]
    for bb in range(nb):
        for j in range(slabs):
            lanes = slice(bb * dn + j * LANES, bb * dn + (j + 1) * LANES)
            y_ref[:, lanes] = y_s[j, pl.ds(bb, tt, stride=nb), :] * _silu(zs_ref[:, lanes])


def _ssm(u, zs, nb, h0_re, h0_im, wb, wc, coef, d_skip, w_glu, b_glu, rows_per_step):
    t, width = u.shape
    dn = width // nb
    gp = wb.shape[1] // 2
    assert nb % SUBLANES == 0 and rows_per_step % nb == 0
    tt = rows_per_step // nb
    assert t % tt == 0
    row = pl.BlockSpec((tt, width), lambda j: (j, 0))
    const = lambda shape: pl.BlockSpec(shape, lambda j: (0,) * len(shape))
    return pl.pallas_call(
        functools.partial(_ssm_kernel, tt=tt, nb=nb, gp=gp),
        grid=(t // tt,),
        in_specs=[row, row, const((nb, gp)), const((nb, gp)), const(wb.shape), const(wc.shape),
                  const(coef.shape), const((1, dn)), const(w_glu.shape), const((1, dn))],
        out_specs=[row, const((nb, gp)), const((nb, gp))],
        out_shape=[jax.ShapeDtypeStruct((t, width), F32),
                   jax.ShapeDtypeStruct((nb, gp), F32),
                   jax.ShapeDtypeStruct((nb, gp), F32)],
        scratch_shapes=[pltpu.VMEM((rows_per_step, 2 * gp), F32),
                        pltpu.VMEM((dn // LANES, rows_per_step, LANES), F32),
                        pltpu.VMEM((dn // LANES, rows_per_step, LANES), F32)],
        compiler_params=_cparams(("arbitrary",)),
        name="ssm_scan",
    )(u, zs, h0_re, h0_im, wb, wc, coef,
      d_skip.reshape(1, dn), w_glu.astype(MXU_DTYPE), b_glu.reshape(1, dn))


def _num_key_blocks(i, tq, tk, nkb, causal):
    if not causal:
        return nkb
    return ((i + 1) * tq + (tk - 1)) // tk


def _key_chunk(tq, tk):
    lane_tiles = -(-tq // LANES)
    for sub in (256, 128, 64, 32, 16, 8):
        if tk % sub == 0 and (sub // SUBLANES) * lane_tiles <= VREGS // 2:
            return sub
    raise AssertionError((tq, tk))


def _select_kernel(qit_ref, wit_ref, ki_ref, bias_ref, keys_s, hi_s, lo_s, lg_s, *,
                   tq, tk, nkb, causal, n_valid, n_sel):
    i = pl.program_id(1)
    nb = _num_key_blocks(i, tq, tk, nkb, causal)
    if causal:
        q_t = i * tq + lax.broadcasted_iota(I32, (1, tq), 1)
        limit = (q_t // CHUNK + 1) * CHUNK
    else:
        limit = n_valid
    wit = wit_ref[0]

    def key_index(j):
        return j * tk + lax.broadcasted_iota(I32, (tk, tq), 0)

    sub = _key_chunk(tq, tk)

    def score_block(j, masked):
        for cc in range(tk // sub):
            rows = slice(cc * sub, (cc + 1) * sub)
            kb = ki_ref[0, j, rows, :]

            def logits(hh):
                return jnp.dot(kb, qit_ref[0, hh * IDX_DIM:(hh + 1) * IDX_DIM, :], preferred_element_type=F32)

            lg_s[0] = logits(0)
            s = jnp.zeros((sub, tq), F32)
            for hh in range(IDX_HEADS):
                if hh + 1 < IDX_HEADS:
                    lg_s[(hh + 1) % 2] = logits(hh + 1)
                s = s + wit[hh:hh + 1, :] * jnp.maximum(lg_s[hh % 2], 0.0)
            bits = pltpu.bitcast(s, I32)
            sign = bits >> 31
            key = (bits ^ (sign & 0x7FFFFFFF)) - sign
            if masked:
                k_idx = j * tk + cc * sub + lax.broadcasted_iota(I32, (sub, tq), 0)
                key = jnp.where(k_idx < limit, key, INT_MIN)
            keys_s[j, rows, :] = key
            hi_s[j, rows, :] = (key >> 16).astype(jnp.int16)

    n_full = ((i * tq) // CHUNK + 1) * CHUNK // tk if causal else n_valid // tk

    def full_block(j, c):
        score_block(j, False)
        return c

    def edge_block(j, c):
        score_block(j, True)
        return c

    lax.fori_loop(0, n_full, full_block, 0)
    lax.fori_loop(n_full, nb, edge_block, 0)

    def count(pred_fns):
        def body(j, accs):
            key = keys_s[j]
            return tuple(acc + jnp.sum(jnp.where(fn(key), 1.0, 0.0).reshape(tk // COUNT_ROWS, COUNT_ROWS, tq),
                                       axis=0)
                         for fn, acc in zip(pred_fns, accs))
        accs = lax.fori_loop(0, nb, body, tuple(jnp.zeros((COUNT_ROWS, tq), F32) for _ in pred_fns))
        return [jnp.sum(a, axis=0, keepdims=True) for a in accs]

    def count16(ref, thr16, strict):
        def body(j, acc):
            blk = ref[j]
            hit = blk > thr16 if strict else blk >= thr16
            ind = jnp.where(hit, jnp.int16(1), jnp.int16(0))
            part = ind[0:COUNT_ROWS]
            for r in range(1, tk // COUNT_ROWS):
                part = part + ind[r * COUNT_ROWS:(r + 1) * COUNT_ROWS]
            return acc + part.astype(F32)
        acc = lax.fori_loop(0, nb, body, jnp.zeros((COUNT_ROWS, tq), F32))
        return jnp.sum(acc, axis=0, keepdims=True)

    def radix16(ref, need, count_all):
        def bit_step(bi, carry):
            u, c_acc = carry
            cand = u | lax.shift_left(jnp.int32(1), 15 - bi)
            cnt = count16(ref, (cand + INT16_MIN).astype(jnp.int16), False)
            keep = cnt >= need
            return jnp.where(keep, cand, u), jnp.where(keep, cnt, c_acc)
        u, c_acc = lax.fori_loop(0, 16, bit_step, (jnp.zeros((1, tq), I32), count_all))
        return u + INT16_MIN, c_acc

    n_adm = jnp.broadcast_to(jnp.asarray(limit, F32), (1, tq))
    n_all = jnp.broadcast_to((nb * tk).astype(F32) if causal else jnp.float32(nb * tk), (1, tq))
    hi_sel, c_ge_hi = radix16(hi_s, n_sel, n_all)
    c_gt_hi = count16(hi_s, hi_sel.astype(jnp.int16), True)
    want_lo = n_sel - c_gt_hi

    def low_block(j, c):
        key = keys_s[j]
        low = (key & 0xFFFF) + INT16_MIN
        lo_s[j] = jnp.where((key >> 16) == hi_sel, low, INT16_MIN).astype(jnp.int16)
        return c

    lax.fori_loop(0, nb, low_block, 0)
    lo_sel, c_ge_lo = radix16(lo_s, want_lo, c_ge_hi - c_gt_hi)
    kth = lax.shift_left(hi_sel, 16) | (lo_sel - INT16_MIN)
    c_ge = jnp.where(kth == INT_MIN, n_adm, c_gt_hi + c_ge_lo)
    thr = jnp.maximum(kth, INT_MIN + 1)
    tie_q = c_ge > n_sel
    has_tie = jnp.max(jnp.where(tie_q, 1.0, 0.0)) > 0.0

    @pl.when(jnp.logical_not(has_tie))
    def _():
        def bias_block(j, c):
            bias_ref[0, 0, j] = jnp.where(keys_s[j] >= thr, 0.0, MASK_NEG).astype(bias_ref.dtype)
            return c

        lax.fori_loop(0, nb, bias_block, 0)

    @pl.when(has_tie)
    def _():
        (c_gt,) = count([lambda kk: kk > thr])
        want = n_sel - c_gt
        tri = jnp.where(lax.broadcasted_iota(I32, (tk, tk), 0) >= lax.broadcasted_iota(I32, (tk, tk), 1),
                        1.0, 0.0).astype(jnp.bfloat16)

        def body(j, carry):
            before, below = carry
            eq = jnp.where(keys_s[j] == thr, 1.0, 0.0).astype(jnp.bfloat16)
            pref = before + jnp.dot(tri, eq, preferred_element_type=F32)
            below = below + jnp.sum(jnp.where(pref < want, 1.0, 0.0), axis=0, keepdims=True)
            return pref[tk - 1:tk, :], below

        _, below = lax.fori_loop(0, nb, body, (jnp.zeros((1, tq), F32), jnp.zeros((1, tq), F32)))
        jstar = jnp.where(tie_q, below.astype(I32), NO_TIE_LIMIT)

        def bias_block(j, c):
            key = keys_s[j]
            tie_ok = jnp.where(key_index(j) <= jstar, 0.0, MASK_NEG)
            bias = jnp.where(key > thr, 0.0, jnp.where(key == thr, tie_ok, MASK_NEG))
            bias_ref[0, 0, j] = bias.astype(bias_ref.dtype)
            return c

        lax.fori_loop(0, nb, bias_block, 0)

    def masked_block(j, c):
        bias_ref[0, 0, j] = jnp.full((tk, tq), MASK_NEG, bias_ref.dtype)
        return c

    lax.fori_loop(nb, nkb, masked_block, 0)


def _select(qit, wit, ki_blk, tq, causal, n_valid, n_sel):
    b, width, t = qit.shape
    _, nkb, tk, _ = ki_blk.shape
    nq = t // tq
    return pl.pallas_call(
        functools.partial(_select_kernel, tq=tq, tk=tk, nkb=nkb, causal=causal, n_valid=n_valid, n_sel=n_sel),
        grid=(b, nq),
        in_specs=[pl.BlockSpec((1, width, tq), lambda i, j: (i, 0, j)),
                  pl.BlockSpec((1, IDX_HEADS, tq), lambda i, j: (i, 0, j)),
                  pl.BlockSpec((1, nkb, tk, IDX_DIM), lambda i, j: (i, 0, 0, 0))],
        out_specs=pl.BlockSpec((1, 1, nkb, tk, tq), lambda i, j: (i, j, 0, 0, 0)),
        out_shape=jax.ShapeDtypeStruct((b, nq, nkb, tk, tq), jnp.bfloat16),
        scratch_shapes=[pltpu.VMEM((nkb, tk, tq), I32), pltpu.VMEM((nkb, tk, tq), jnp.int16),
                        pltpu.VMEM((nkb, tk, tq), jnp.int16), pltpu.VMEM((2, _key_chunk(tq, tk), tq), F32)],
        compiler_params=_cparams(("arbitrary", "arbitrary")),
        name="topk_select",
    )(qit, wit, ki_blk)


def _attn_kernel(qtile_ref, kblock_ref, qt_ref, k_ref, vt_ref, bias_ref, o_ref, m_s, l_s, acc_s, s_s, bias_s, *,
                 tq, tk, nkb, causal):
    step = pl.program_id(1)
    j = kblock_ref[step]
    last = _num_key_blocks(qtile_ref[step], tq, tk, nkb, causal) - 1

    @pl.when(j == 0)
    def _():
        m_s[...] = jnp.full(m_s.shape, M_INIT, F32)
        l_s[...] = jnp.zeros(l_s.shape, F32)
        acc_s[...] = jnp.zeros(acc_s.shape, F32)

    sub = _key_chunk(tq, tk)
    chunks = [slice(c * sub, (c + 1) * sub) for c in range(tk // sub)]
    m_all = m_s[...]
    l_all = l_s[...]
    bias_s[...] = bias_ref[0, 0, 0].astype(F32)

    def stage_a(hh, keys, mx):
        s = jnp.dot(k_ref[0, hh, keys, :], qt_ref[0, hh * HEAD_DIM:(hh + 1) * HEAD_DIM, :],
                    preferred_element_type=F32) + bias_s[keys, :]
        s_s[hh % 2, keys, :] = s
        return jnp.maximum(mx, jnp.max(s, axis=0, keepdims=True))

    def stage_b(hh, keys, m_new, lsum, pv):
        p = jnp.exp2(s_s[hh % 2, keys, :] - m_new)
        lsum = lsum + jnp.sum(p, axis=0, keepdims=True)
        pv = pv + jnp.dot(vt_ref[0, hh * HEAD_DIM:(hh + 1) * HEAD_DIM, keys], p.astype(MXU_DTYPE),
                          preferred_element_type=F32)
        return lsum, pv

    m_new = m_all[0:1, :]
    for keys in chunks:
        m_new = stage_a(0, keys, m_new)
    m_out, l_out = [], []
    for hh in range(N_HEADS):
        rows = slice(hh * HEAD_DIM, (hh + 1) * HEAD_DIM)
        lsum = jnp.zeros((1, tq), F32)
        pv = jnp.zeros((HEAD_DIM, tq), F32)
        m_next = m_all[hh + 1:hh + 2, :] if hh + 1 < N_HEADS else None
        for keys in chunks:
            if m_next is not None:
                m_next = stage_a(hh + 1, keys, m_next)
            lsum, pv = stage_b(hh, keys, m_new, lsum, pv)
        alpha = jnp.exp2(m_all[hh:hh + 1, :] - m_new)
        l_out.append(alpha * l_all[hh:hh + 1, :] + lsum)
        acc_s[rows, :] = alpha * acc_s[rows, :] + pv
        m_out.append(m_new)
        m_new = m_next
    m_s[...] = jnp.concatenate(m_out, axis=0)
    l_s[...] = jnp.concatenate(l_out, axis=0)

    @pl.when(j == last)
    def _():
        out_t = jnp.concatenate(
            [acc_s[hh * HEAD_DIM:(hh + 1) * HEAD_DIM, :] / l_s[hh:hh + 1, :] for hh in range(N_HEADS)], axis=0)
        o_ref[0] = out_t.T


def _attention(qt, k_hm, vt, bias, tq, tk, causal):
    b, width, t = qt.shape
    s_len = vt.shape[-1]
    nq, nkb = t // tq, s_len // tk
    pairs = [(i, j) for i in range(nq) for j in range(_num_key_blocks(i, tq, tk, nkb, causal))]
    qtile = jnp.asarray([p[0] for p in pairs], I32)
    kblock = jnp.asarray([p[1] for p in pairs], I32)
    grid_spec = pltpu.PrefetchScalarGridSpec(
        num_scalar_prefetch=2,
        grid=(b, len(pairs)),
        in_specs=[pl.BlockSpec((1, width, tq), lambda bb, s, qi, kj: (bb, 0, qi[s])),
                  pl.BlockSpec((1, N_HEADS, tk, HEAD_DIM), lambda bb, s, qi, kj: (bb, 0, kj[s], 0)),
                  pl.BlockSpec((1, width, tk), lambda bb, s, qi, kj: (bb, 0, kj[s])),
                  pl.BlockSpec((1, 1, 1, tk, tq), lambda bb, s, qi, kj: (bb, qi[s], kj[s], 0, 0))],
        out_specs=pl.BlockSpec((1, tq, width), lambda bb, s, qi, kj: (bb, qi[s], 0)),
        scratch_shapes=[pltpu.VMEM((N_HEADS, tq), F32), pltpu.VMEM((N_HEADS, tq), F32),
                        pltpu.VMEM((width, tq), F32), pltpu.VMEM((2, tk, tq), F32),
                        pltpu.VMEM((tk, tq), F32)])
    return pl.pallas_call(
        functools.partial(_attn_kernel, tq=tq, tk=tk, nkb=nkb, causal=causal),
        grid_spec=grid_spec,
        out_shape=jax.ShapeDtypeStruct((b, t, width), F32),
        compiler_params=_cparams(("arbitrary", "arbitrary")),
        name="masked_attention",
    )(qtile, kblock, qt, k_hm, vt, bias)


def _merge_kernel(x_ref, ys_ref, att_ref, za_ref, gs_ref, ga_ref, gate_ref, wps_ref, wpa_ref, wo_ref,
                  gfin_ref, o_ref, *, final_norm):
    ya = att_ref[0] * _silu(za_ref[0])
    merged = (jax.nn.sigmoid(gs_ref[0]) * _mxu_dot(ys_ref[...], wps_ref[...])
              + jax.nn.sigmoid(ga_ref[0]) * _mxu_dot(ya, wpa_ref[...]))
    x_new = x_ref[0] + gate_ref[0] * _mxu_dot(merged, wo_ref[...])
    if final_norm:
        x_new = x_new * lax.rsqrt(jnp.mean(x_new * x_new, axis=-1, keepdims=True) + EPS) * gfin_ref[...]
    o_ref[0] = x_new


def _merge(x, ys, att, za, gs, ga, gate, w_ps, w_pa, w_o, g_final, tm, final_norm):
    b, t, d = x.shape
    half = d // 2
    row = lambda width: pl.BlockSpec((1, tm, width), lambda i, j: (i, j, 0))
    const = lambda shape: pl.BlockSpec(shape, lambda i, j: (0,) * len(shape), pipeline_mode=pl.Buffered(1))
    return pl.pallas_call(
        functools.partial(_merge_kernel, final_norm=final_norm),
        grid=(b, t // tm),
        in_specs=[row(d), pl.BlockSpec((tm, half), lambda i, j: (j, i)), row(half), row(half), row(d), row(d),
                  pl.BlockSpec((1, 1, d), lambda i, j: (i, 0, 0)),
                  const((half, d)), const((half, d)), const((d, d)), const((1, d))],
        out_specs=row(d),
        out_shape=jax.ShapeDtypeStruct((b, t, d), F32),
        compiler_params=_cparams(("arbitrary", "arbitrary")),
        name="merge_out",
    )(x, ys, att, za, gs, ga, gate, w_ps.astype(MXU_DTYPE), w_pa.astype(MXU_DTYPE),
      w_o.astype(MXU_DTYPE), g_final.reshape(1, d))


def _row_tile(t, cap):
    tile = min(t, cap)
    assert t % tile == 0, (t, tile)
    return tile


def _key_blocks(s_len):
    s_pad = -(-s_len // LANES) * LANES
    for tk in (512, 384, 256, 128):
        if s_pad % tk == 0:
            return s_pad, tk
    raise AssertionError(s_len)


def _layer(x, mod, lw, ssm_par, h0_re, h0_im, past, final_norm):
    b, t, d = x.shape
    shift, scale, gate = (m.reshape(b, 1, d) for m in jnp.split(mod, 3, axis=-1))
    tm = _row_tile(t, PROJ_ROW_TILE)
    (u, zs, za, gs, ga, k, v, ki, wit, qt, qit, vt, k_hm) = _inproj(
        x, shift, scale, lw["g_norm"], lw["wmain"], lw["wsmall"], tm)

    wb, wc, coef = ssm_par
    ys, hr, hi = _ssm(u, zs, b, h0_re, h0_im, wb, wc, coef,
                      lw["d_skip"], lw["w_glu"], lw["b_glu"], min(t * b, PROJ_ROW_TILE))

    ki_bf = ki.astype(MXU_DTYPE)
    if past is None:
        causal, n_valid = True, t
        s_pad, tk = _key_blocks(n_valid)
        assert s_pad == n_valid
    else:
        past_k, past_v, past_ki = past
        plen = past_ki.shape[1]
        causal, n_valid = False, plen + t
        s_pad, tk = _key_blocks(n_valid)
        grow = lambda a, axis: jnp.pad(a.astype(MXU_DTYPE),
                                       [(0, s_pad - plen if ax == axis else 0) for ax in range(a.ndim)])
        k_hm = lax.dynamic_update_slice(grow(past_k.transpose(0, 2, 1, 3), 2), k_hm, (0, 0, plen, 0))
        vt = lax.dynamic_update_slice(grow(past_v.reshape(b, plen, d // 2).transpose(0, 2, 1), 2), vt,
                                      (0, 0, plen))
        ki_bf = lax.dynamic_update_slice(grow(past_ki, 1), ki_bf, (0, plen, 0))
    n_sel = min(TOPK_MAX, n_valid // 4)
    ki_blk = ki_bf.reshape(b, s_pad // tk, tk, IDX_DIM)

    tq = _row_tile(t, ROW_TILE)
    bias = _select(qit, wit, ki_blk, tq, causal, n_valid, n_sel)
    att = _attention(qt, k_hm, vt, bias, tq, tk, causal)

    x_new = _merge(x, ys, att, za, gs, ga, gate, lw["w_ps"], lw["w_pa"], lw["w_o"], lw["g_final"],
                   _row_tile(t, PROJ_ROW_TILE), final_norm)
    gshape = (b, -1, STATE)
    return x_new, (k.reshape(b, t, N_HEADS, HEAD_DIM), v.reshape(b, t, N_HEADS, HEAD_DIM), ki,
                   hr.reshape(gshape), hi.reshape(gshape))


def kernel(x_prompt, x_sample, cache_k, cache_v, cache_kidx, state_ssm_re, state_ssm_im, c_prompt, c_sample, w_mod, b_mod, g_norm, w_in, a_re, a_im, log_dt, b_re, b_im, c_re, c_im, d_skip, w_glu, b_glu, w_ps, w_pa, w_o, g_final):
    depth = w_mod.shape[0]
    bp, _, d = x_prompt.shape
    g, p = a_re.shape[1:]
    mod = _modulation(jnp.concatenate([c_prompt, c_sample], axis=0), w_mod, b_mod)
    zeros = jnp.zeros((bp, g * p), F32)
    xp, xs = x_prompt, x_sample
    outs_p, outs_s = [], []
    for l in range(depth):
        wmain, wsmall = _inproj_weights(w_in[l], d)
        lw = dict(g_norm=g_norm[l], wmain=wmain, wsmall=wsmall, d_skip=d_skip[l], w_glu=w_glu[l],
                  b_glu=b_glu[l], w_ps=w_ps[l], w_pa=w_pa[l], w_o=w_o[l], g_final=g_final)
        ssm_par = _ssm_prep(a_re[l], a_im[l], log_dt[l], b_re[l], b_im[l], c_re[l], c_im[l])
        last = l == depth - 1
        xp, op = _layer(xp, mod[l, :bp], lw, ssm_par, zeros, zeros, None, last)
        xs, os_ = _layer(xs, mod[l, bp:], lw, ssm_par, state_ssm_re[l].reshape(-1, g * p),
                         state_ssm_im[l].reshape(-1, g * p), (cache_k[l], cache_v[l], cache_kidx[l]), last)
        outs_p.append(op)
        outs_s.append(os_)
    stack = lambda outs: tuple(jnp.stack(z) for z in zip(*outs))
    return (xp, xs) + stack(outs_p) + stack(outs_s)
```

```python
import functools
import math

import jax
import jax.numpy as jnp
from jax import lax
from jax.experimental import pallas as pl
from jax.experimental.pallas import tpu as pltpu

F32 = jnp.float32
I32 = jnp.int32
MXU_DTYPE = jnp.bfloat16

CHUNK = 64
SSM_GROUP = 16
STATE = 64
N_HEADS = 8
HEAD_DIM = 64
IDX_HEADS = 8
IDX_DIM = 64
TOPK_MAX = 256
EPS = 1e-6
LOG2E = math.log2(math.e)

LANES = 128
SUBLANES = 8
VREGS = 64
ROW_TILE = 256
PROJ_ROW_TILE = 512
V7X_VMEM_BYTES = 64 * 1024 * 1024
VMEM_LIMIT = V7X_VMEM_BYTES - 8 * 1024 * 1024

INT_MIN = -(2 ** 31)
INT16_MIN = -(2 ** 15)
MASK_NEG = -1e30
M_INIT = -1e29
NO_TIE_LIMIT = 2 ** 30
COUNT_ROWS = 32


def _cparams(sem):
    return pltpu.CompilerParams(dimension_semantics=sem, vmem_limit_bytes=VMEM_LIMIT)


def _silu(x):
    return x * jax.nn.sigmoid(x)


def _mxu_dot(a, b):
    return jnp.dot(a.astype(MXU_DTYPE), b.astype(MXU_DTYPE), preferred_element_type=F32)


def _split_hi_lo(a):
    hi = a.astype(jnp.bfloat16)
    lo = (a - hi.astype(F32)).astype(jnp.bfloat16)
    return hi, lo


def _dot_3pass(a, b):
    ah, al = _split_hi_lo(a)
    bh, bl = _split_hi_lo(b)
    d = functools.partial(jnp.dot, preferred_element_type=F32)
    return d(ah, bh) + (d(ah, bl) + d(al, bh))


def _mod_kernel(c_ref, w_ref, b_ref, o_ref):
    o_ref[...] = _dot_3pass(_silu(c_ref[...]), w_ref[...]) + b_ref[...]


def _modulation(c_all, w_mod, b_mod):
    depth, d, d3 = w_mod.shape
    rows = c_all.shape[0]
    cols = d3 // 4
    return pl.pallas_call(
        _mod_kernel,
        grid=(depth, d3 // cols),
        in_specs=[
            pl.BlockSpec((rows, d), lambda l, j: (0, 0)),
            pl.BlockSpec((None, d, cols), lambda l, j: (l, 0, j)),
            pl.BlockSpec((None, 1, cols), lambda l, j: (l, 0, j)),
        ],
        out_specs=pl.BlockSpec((None, rows, cols), lambda l, j: (l, 0, j)),
        out_shape=jax.ShapeDtypeStruct((depth, rows, d3), F32),
        compiler_params=_cparams(("arbitrary", "arbitrary")),
        name="adaln_mod",
    )(c_all, w_mod, b_mod.reshape(depth, 1, d3))


def _ssm_prep_kernel(are_ref, aim_ref, ldt_ref, bre_ref, bim_ref, cre_ref, cim_ref,
                     wb_ref, wc_ref, coef_ref):
    a_re = are_ref[...]
    a_im = aim_ref[...]
    dt = jnp.exp(ldt_ref[...])
    gp = a_re.shape[-1]

    mag = jnp.exp(a_re * dt)
    ang = a_im * dt
    ab_re, ab_im = mag * jnp.cos(ang), mag * jnp.sin(ang)
    den = a_re * a_re + a_im * a_im
    nr = ab_re - 1.0
    ni = ab_im
    f_re = (nr * a_re + ni * a_im) / den
    f_im = (ni * a_re - nr * a_im) / den

    rows_b = lax.broadcasted_iota(I32, bre_ref.shape, 0) // SSM_GROUP
    cols_b = lax.broadcasted_iota(I32, bre_ref.shape, 1) // STATE
    on_b = rows_b == cols_b
    b_re = bre_ref[...]
    b_im = bim_ref[...]
    wb_ref[:, 0:gp] = jnp.where(on_b, f_re * b_re - f_im * b_im, 0.0).astype(wb_ref.dtype)
    wb_ref[:, gp:2 * gp] = jnp.where(on_b, f_re * b_im + f_im * b_re, 0.0).astype(wb_ref.dtype)

    rows_c = lax.broadcasted_iota(I32, cre_ref.shape, 0) // STATE
    cols_c = lax.broadcasted_iota(I32, cre_ref.shape, 1) // SSM_GROUP
    on_c = rows_c == cols_c
    wc_ref[0:gp, :] = jnp.where(on_c, cre_ref[...], 0.0).astype(wc_ref.dtype)
    wc_ref[gp:2 * gp, :] = jnp.where(on_c, -cim_ref[...], 0.0).astype(wc_ref.dtype)

    coef_ref[0] = jnp.broadcast_to(ab_re, (SUBLANES, gp))
    coef_ref[1] = jnp.broadcast_to(ab_im, (SUBLANES, gp))


def _ssm_prep(a_re, a_im, log_dt, b_re, b_im, c_re, c_im):
    g, p = a_re.shape
    n = b_re.shape[-1]
    gp, gn = g * p, g * n
    row = lambda a: a.reshape(1, gp)
    ldt = jnp.broadcast_to(log_dt[:, None], (g, p)).reshape(1, gp)
    b_exp = lambda b: jnp.broadcast_to(b.transpose(2, 0, 1).reshape(1, n, gp), (g, n, gp)).reshape(gn, gp)
    c_exp = lambda c: jnp.broadcast_to(c.transpose(0, 2, 1).reshape(gp, 1, n), (gp, g, n)).reshape(gp, gn)
    return pl.pallas_call(
        _ssm_prep_kernel,
        out_shape=(jax.ShapeDtypeStruct((gn, 2 * gp), MXU_DTYPE),
                   jax.ShapeDtypeStruct((2 * gp, gn), MXU_DTYPE),
                   jax.ShapeDtypeStruct((2, SUBLANES, gp), F32)),
        compiler_params=pltpu.CompilerParams(vmem_limit_bytes=VMEM_LIMIT),
        name="ssm_prep",
    )(row(a_re), row(a_im), ldt, b_exp(b_re), b_exp(b_im), c_exp(c_re), c_exp(c_im))


_MAIN_SEGS = ("u", "zs", "za", "gs", "ga", "k", "v", "q", "qi")


def _main_offsets(d):
    half = d // 2
    widths = dict(u=half, zs=half, za=half, gs=d, ga=d, k=half, v=half, q=half, qi=half)
    offs, o = {}, 0
    for name in _MAIN_SEGS:
        offs[name] = (o, o + widths[name])
        o += widths[name]
    return offs, o


def _inproj_kernel(x_ref, shift_ref, scale_ref, g_ref, wmain_ref, wsmall_ref,
                   u_ref, zs_ref, za_ref, gs_ref, ga_ref, k_ref, v_ref, ki_ref,
                   wit_ref, qt_ref, qit_ref, vt_ref, khm_ref, *, offs):
    x = x_ref[0]
    h = x * lax.rsqrt(jnp.mean(x * x, axis=-1, keepdims=True) + EPS) * g_ref[...]
    h = h * (1.0 + scale_ref[0]) + shift_ref[0]
    hb = h.astype(MXU_DTYPE)

    def proj(name):
        c0, c1 = offs[name]
        return jnp.dot(hb, wmain_ref[:, c0:c1], preferred_element_type=F32)

    u_ref[...] = proj("u")
    zs_ref[...] = proj("zs")
    za_ref[0] = proj("za")
    gs_ref[0] = proj("gs")
    ga_ref[0] = proj("ga")

    zk = proj("k")
    k_ref[0] = zk
    for hh in range(N_HEADS):
        khm_ref[0, hh] = zk[:, hh * HEAD_DIM:(hh + 1) * HEAD_DIM].astype(khm_ref.dtype)
    zv = proj("v")
    v_ref[0] = zv
    vt_ref[0] = zv.T.astype(vt_ref.dtype)
    qt_ref[0] = proj("q").T.astype(qt_ref.dtype)
    qit_ref[0] = proj("qi").T.astype(qit_ref.dtype)

    zsm = jnp.dot(hb, wsmall_ref[...], preferred_element_type=F32)
    ki_ref[0] = zsm[:, 0:IDX_DIM]
    wit_ref[0] = zsm.T[IDX_DIM:IDX_DIM + IDX_HEADS, :] * (IDX_HEADS ** -0.5)


def _inproj_weights(w_in, d):
    half = d // 2
    idx_w = IDX_HEADS * IDX_DIM
    o = 0
    seg = {}
    for name, width in (("u", half), ("zs", half), ("q", half), ("k", half), ("v", half), ("za", half),
                        ("qi", idx_w), ("ki", IDX_DIM), ("wi", IDX_HEADS), ("gs", d), ("ga", d)):
        seg[name] = w_in[:, o:o + width]
        o += width
    seg["q"] = seg["q"] * (HEAD_DIM ** -0.5 * LOG2E)
    seg["qi"] = seg["qi"] * (IDX_DIM ** -0.5)
    wmain = jnp.concatenate([seg[n] for n in _MAIN_SEGS], axis=1).astype(MXU_DTYPE)
    wsmall = jnp.concatenate(
        [seg["ki"], seg["wi"], jnp.zeros((d, LANES - IDX_DIM - IDX_HEADS), w_in.dtype)], axis=1).astype(MXU_DTYPE)
    return wmain, wsmall


def _inproj(x, shift, scale, g_norm, wmain, wsmall, tm):
    b, t, d = x.shape
    half = d // 2
    offs, wcols = _main_offsets(d)
    nt = t // tm
    row = lambda width: pl.BlockSpec((1, tm, width), lambda i, j: (i, j, 0))
    hm = pl.BlockSpec((1, N_HEADS, tm, HEAD_DIM), lambda i, j: (i, 0, j, 0))
    tr = lambda rows: pl.BlockSpec((1, rows, tm), lambda i, j: (i, 0, j))
    tb = pl.BlockSpec((tm, half), lambda i, j: (j, i))
    per_b = pl.BlockSpec((1, 1, d), lambda i, j: (i, 0, 0))
    const = lambda shape: pl.BlockSpec(shape, lambda i, j: (0,) * len(shape), pipeline_mode=pl.Buffered(1))
    f = lambda *s: jax.ShapeDtypeStruct(s, F32)
    bf = lambda *s: jax.ShapeDtypeStruct(s, MXU_DTYPE)
    return pl.pallas_call(
        functools.partial(_inproj_kernel, offs=offs),
        grid=(b, nt),
        in_specs=[row(d), per_b, per_b, const((1, d)), const((d, wcols)), const((d, LANES))],
        out_specs=[tb, tb, row(half), row(d), row(d), row(half), row(half),
                   row(IDX_DIM), tr(IDX_HEADS), tr(half), tr(half), tr(half), hm],
        out_shape=[f(t, b * half), f(t, b * half), f(b, t, half), f(b, t, d), f(b, t, d),
                   f(b, t, half), f(b, t, half), f(b, t, IDX_DIM), f(b, IDX_HEADS, t),
                   bf(b, half, t), bf(b, half, t), bf(b, half, t), bf(b, N_HEADS, t, HEAD_DIM)],
        compiler_params=_cparams(("arbitrary", "arbitrary")),
        name="inproj",
    )(x, shift, scale, g_norm.reshape(1, d), wmain, wsmall)


def _ssm_kernel(u_ref, zs_ref, h0r_ref, h0i_ref, wb_ref, wc_ref, coef_ref, d_ref, wglu_ref, bglu_ref,
                y_ref, hr_ref, hi_ref, bu_s, u_s, y_s, *, tt, nb, gp):
    @pl.when(pl.program_id(0) == 0)
    def _():
        hr_ref[...] = h0r_ref[...]
        hi_ref[...] = h0i_ref[...]

    slabs = u_s.shape[0]
    dn = slabs * LANES
    for bb in range(nb):
        for j in range(slabs):
            lanes = slice(bb * dn + j * LANES, bb * dn + (j + 1) * LANES)
            u_s[j, pl.ds(bb, tt, stride=nb), :] = u_ref[:, lanes]
    u = jnp.concatenate([u_s[j] for j in range(slabs)], axis=-1)
    ub = u.astype(MXU_DTYPE)
    sw = gp // slabs
    for j in range(slabs):
        ch = slice(j * LANES, (j + 1) * LANES)
        for part in range(2):
            st = slice(part * gp + j * sw, part * gp + (j + 1) * sw)
            bu_s[:, st] = jnp.dot(ub[:, ch], wb_ref[ch, st], preferred_element_type=F32)

    groups = nb // SUBLANES

    def step(t, carry):
        a_re = coef_ref[0]
        a_im = coef_ref[1]
        out = []
        for g in range(groups):
            h_re, h_im = carry[2 * g], carry[2 * g + 1]
            rows = pl.ds(pl.multiple_of(t * nb + g * SUBLANES, SUBLANES), SUBLANES)
            n_re = (a_re * h_re - a_im * h_im) + bu_s[rows, 0:gp]
            n_im = (a_re * h_im + a_im * h_re) + bu_s[rows, gp:2 * gp]
            bu_s[rows, 0:gp] = n_re
            bu_s[rows, gp:2 * gp] = n_im
            out += [n_re, n_im]
        return tuple(out)

    init = []
    for g in range(groups):
        rows = slice(g * SUBLANES, (g + 1) * SUBLANES)
        init += [hr_ref[rows, :], hi_ref[rows, :]]
    fin = lax.fori_loop(0, tt, step, tuple(init), unroll=2)
    for g in range(groups):
        rows = slice(g * SUBLANES, (g + 1) * SUBLANES)
        hr_ref[rows, :] = fin[2 * g]
        hi_ref[rows, :] = fin[2 * g + 1]

    ys = []
    for j in range(slabs):
        ch = slice(j * LANES, (j + 1) * LANES)
        acc = d_ref[:, ch] * u[:, ch]
        for part in range(2):
            st = slice(part * gp + j * sw, part * gp + (j + 1) * sw)
            acc = acc + jnp.dot(bu_s[:, st].astype(MXU_DTYPE), wc_ref[st, ch], preferred_element_type=F32)
        ys.append(acc)
    y = jax.nn.gelu(jnp.concatenate(ys, axis=-1))
    y = y * jax.nn.sigmoid(_mxu_dot(y, wglu_ref[...]) + bglu_ref[...])
    for j in range(slabs):
        y_s[j] = y[:, j * LANES:(j + 1) * LANES]
    for bb in range(nb):
        for j in range(slabs):
            lanes = slice(bb * dn + j * LANES, bb * dn + (j + 1) * LANES)
            y_ref[:, lanes] = y_s[j, pl.ds(bb, tt, stride=nb), :] * _silu(zs_ref[:, lanes])


def _ssm(u, zs, nb, h0_re, h0_im, wb, wc, coef, d_skip, w_glu, b_glu, rows_per_step):
    t, width = u.shape
    dn = width // nb
    gp = wb.shape[1] // 2
    assert nb % SUBLANES == 0 and rows_per_step % nb == 0
    tt = rows_per_step // nb
    assert t % tt == 0
    row = pl.BlockSpec((tt, width), lambda j: (j, 0))
    const = lambda shape: pl.BlockSpec(shape, lambda j: (0,) * len(shape))
    return pl.pallas_call(
        functools.partial(_ssm_kernel, tt=tt, nb=nb, gp=gp),
        grid=(t // tt,),
        in_specs=[row, row, const((nb, gp)), const((nb, gp)), const(wb.shape), const(wc.shape),
                  const(coef.shape), const((1, dn)), const(w_glu.shape), const((1, dn))],
        out_specs=[row, const((nb, gp)), const((nb, gp))],
        out_shape=[jax.ShapeDtypeStruct((t, width), F32),
                   jax.ShapeDtypeStruct((nb, gp), F32),
                   jax.ShapeDtypeStruct((nb, gp), F32)],
        scratch_shapes=[pltpu.VMEM((rows_per_step, 2 * gp), F32),
                        pltpu.VMEM((dn // LANES, rows_per_step, LANES), F32),
                        pltpu.VMEM((dn // LANES, rows_per_step, LANES), F32)],
        compiler_params=_cparams(("arbitrary",)),
        name="ssm_scan",
    )(u, zs, h0_re, h0_im, wb, wc, coef,
      d_skip.reshape(1, dn), w_glu.astype(MXU_DTYPE), b_glu.reshape(1, dn))


def _num_key_blocks(i, tq, tk, nkb, causal):
    if not causal:
        return nkb
    return ((i + 1) * tq + (tk - 1)) // tk


def _key_chunk(tq, tk):
    lane_tiles = -(-tq // LANES)
    for sub in (256, 128, 64, 32, 16, 8):
        if tk % sub == 0 and (sub // SUBLANES) * lane_tiles <= VREGS // 2:
            return sub
    raise AssertionError((tq, tk))


def _select_kernel(qit_ref, wit_ref, ki_ref, bias_ref, keys_s, hi_s, lo_s, lg_s, *,
                   tq, tk, nkb, causal, n_valid, n_sel):
    i = pl.program_id(1)
    nb = _num_key_blocks(i, tq, tk, nkb, causal)
    if causal:
        q_t = i * tq + lax.broadcasted_iota(I32, (1, tq), 1)
        limit = (q_t // CHUNK + 1) * CHUNK
    else:
        limit = n_valid
    wit = wit_ref[0]

    def key_index(j):
        return j * tk + lax.broadcasted_iota(I32, (tk, tq), 0)

    sub = _key_chunk(tq, tk)

    def score_block(j, masked):
        for cc in range(tk // sub):
            rows = slice(cc * sub, (cc + 1) * sub)
            kb = ki_ref[0, j, rows, :]

            def logits(hh):
                return jnp.dot(kb, qit_ref[0, hh * IDX_DIM:(hh + 1) * IDX_DIM, :], preferred_element_type=F32)

            lg_s[0] = logits(0)
            s = jnp.zeros((sub, tq), F32)
            for hh in range(IDX_HEADS):
                if hh + 1 < IDX_HEADS:
                    lg_s[(hh + 1) % 2] = logits(hh + 1)
                s = s + wit[hh:hh + 1, :] * jnp.maximum(lg_s[hh % 2], 0.0)
            bits = pltpu.bitcast(s, I32)
            sign = bits >> 31
            key = (bits ^ (sign & 0x7FFFFFFF)) - sign
            if masked:
                k_idx = j * tk + cc * sub + lax.broadcasted_iota(I32, (sub, tq), 0)
                key = jnp.where(k_idx < limit, key, INT_MIN)
            keys_s[j, rows, :] = key
            hi_s[j, rows, :] = (key >> 16).astype(jnp.int16)

    n_full = ((i * tq) // CHUNK + 1) * CHUNK // tk if causal else n_valid // tk

    def full_block(j, c):
        score_block(j, False)
        return c

    def edge_block(j, c):
        score_block(j, True)
        return c

    lax.fori_loop(0, n_full, full_block, 0)
    lax.fori_loop(n_full, nb, edge_block, 0)

    def count(pred_fns):
        def body(j, accs):
            key = keys_s[j]
            return tuple(acc + jnp.sum(jnp.where(fn(key), 1.0, 0.0).reshape(tk // COUNT_ROWS, COUNT_ROWS, tq),
                                       axis=0)
                         for fn, acc in zip(pred_fns, accs))
        accs = lax.fori_loop(0, nb, body, tuple(jnp.zeros((COUNT_ROWS, tq), F32) for _ in pred_fns))
        return [jnp.sum(a, axis=0, keepdims=True) for a in accs]

    def count16(ref, thr16, strict):
        def body(j, acc):
            blk = ref[j]
            hit = blk > thr16 if strict else blk >= thr16
            ind = jnp.where(hit, jnp.int16(1), jnp.int16(0))
            part = ind[0:COUNT_ROWS]
            for r in range(1, tk // COUNT_ROWS):
                part = part + ind[r * COUNT_ROWS:(r + 1) * COUNT_ROWS]
            return acc + part.astype(F32)
        acc = lax.fori_loop(0, nb, body, jnp.zeros((COUNT_ROWS, tq), F32))
        return jnp.sum(acc, axis=0, keepdims=True)

    def radix16(ref, need, count_all):
        def bit_step(bi, carry):
            u, c_acc = carry
            cand = u | lax.shift_left(jnp.int32(1), 15 - bi)
            cnt = count16(ref, (cand + INT16_MIN).astype(jnp.int16), False)
            keep = cnt >= need
            return jnp.where(keep, cand, u), jnp.where(keep, cnt, c_acc)
        u, c_acc = lax.fori_loop(0, 16, bit_step, (jnp.zeros((1, tq), I32), count_all))
        return u + INT16_MIN, c_acc

    n_adm = jnp.broadcast_to(jnp.asarray(limit, F32), (1, tq))
    n_all = jnp.broadcast_to((nb * tk).astype(F32) if causal else jnp.float32(nb * tk), (1, tq))
    hi_sel, c_ge_hi = radix16(hi_s, n_sel, n_all)
    c_gt_hi = count16(hi_s, hi_sel.astype(jnp.int16), True)
    want_lo = n_sel - c_gt_hi

    def low_block(j, c):
        key = keys_s[j]
        low = (key & 0xFFFF) + INT16_MIN
        lo_s[j] = jnp.where((key >> 16) == hi_sel, low, INT16_MIN).astype(jnp.int16)
        return c

    lax.fori_loop(0, nb, low_block, 0)
    lo_sel, c_ge_lo = radix16(lo_s, want_lo, c_ge_hi - c_gt_hi)
    kth = lax.shift_left(hi_sel, 16) | (lo_sel - INT16_MIN)
    c_ge = jnp.where(kth == INT_MIN, n_adm, c_gt_hi + c_ge_lo)
    thr = jnp.maximum(kth, INT_MIN + 1)
    tie_q = c_ge > n_sel
    has_tie = jnp.max(jnp.where(tie_q, 1.0, 0.0)) > 0.0

    @pl.when(jnp.logical_not(has_tie))
    def _():
        def bias_block(j, c):
            bias_ref[0, 0, j] = jnp.where(keys_s[j] >= thr, 0.0, MASK_NEG).astype(bias_ref.dtype)
            return c

        lax.fori_loop(0, nb, bias_block, 0)

    @pl.when(has_tie)
    def _():
        (c_gt,) = count([lambda kk: kk > thr])
        want = n_sel - c_gt
        tri = jnp.where(lax.broadcasted_iota(I32, (tk, tk), 0) >= lax.broadcasted_iota(I32, (tk, tk), 1),
                        1.0, 0.0).astype(jnp.bfloat16)

        def body(j, carry):
            before, below = carry
            eq = jnp.where(keys_s[j] == thr, 1.0, 0.0).astype(jnp.bfloat16)
            pref = before + jnp.dot(tri, eq, preferred_element_type=F32)
            below = below + jnp.sum(jnp.where(pref < want, 1.0, 0.0), axis=0, keepdims=True)
            return pref[tk - 1:tk, :], below

        _, below = lax.fori_loop(0, nb, body, (jnp.zeros((1, tq), F32), jnp.zeros((1, tq), F32)))
        jstar = jnp.where(tie_q, below.astype(I32), NO_TIE_LIMIT)

        def bias_block(j, c):
            key = keys_s[j]
            tie_ok = jnp.where(key_index(j) <= jstar, 0.0, MASK_NEG)
            bias = jnp.where(key > thr, 0.0, jnp.where(key == thr, tie_ok, MASK_NEG))
            bias_ref[0, 0, j] = bias.astype(bias_ref.dtype)
            return c

        lax.fori_loop(0, nb, bias_block, 0)

    def masked_block(j, c):
        bias_ref[0, 0, j] = jnp.full((tk, tq), MASK_NEG, bias_ref.dtype)
        return c

    lax.fori_loop(nb, nkb, masked_block, 0)


def _select(qit, wit, ki_blk, tq, causal, n_valid, n_sel):
    b, width, t = qit.shape
    _, nkb, tk, _ = ki_blk.shape
    nq = t // tq
    return pl.pallas_call(
        functools.partial(_select_kernel, tq=tq, tk=tk, nkb=nkb, causal=causal, n_valid=n_valid, n_sel=n_sel),
        grid=(b, nq),
        in_specs=[pl.BlockSpec((1, width, tq), lambda i, j: (i, 0, j)),
                  pl.BlockSpec((1, IDX_HEADS, tq), lambda i, j: (i, 0, j)),
                  pl.BlockSpec((1, nkb, tk, IDX_DIM), lambda i, j: (i, 0, 0, 0))],
        out_specs=pl.BlockSpec((1, 1, nkb, tk, tq), lambda i, j: (i, j, 0, 0, 0)),
        out_shape=jax.ShapeDtypeStruct((b, nq, nkb, tk, tq), jnp.bfloat16),
        scratch_shapes=[pltpu.VMEM((nkb, tk, tq), I32), pltpu.VMEM((nkb, tk, tq), jnp.int16),
                        pltpu.VMEM((nkb, tk, tq), jnp.int16), pltpu.VMEM((2, _key_chunk(tq, tk), tq), F32)],
        compiler_params=_cparams(("arbitrary", "arbitrary")),
        name="topk_select",
    )(qit, wit, ki_blk)


def _attn_kernel(qtile_ref, kgroup_ref, qt_ref, k_ref, vt_ref, bias_ref, o_ref, m_s, l_s, acc_s, s_s, bias_s, *,
                 tq, tk, nkb, causal, kps):
    step = pl.program_id(1)
    group = kgroup_ref[step]
    nb = _num_key_blocks(qtile_ref[step], tq, tk, nkb, causal)

    @pl.when(group == 0)
    def _():
        m_s[...] = jnp.full(m_s.shape, M_INIT, F32)
        l_s[...] = jnp.zeros(l_s.shape, F32)
        acc_s[...] = jnp.zeros(acc_s.shape, F32)

    sub = _key_chunk(tq, tk)

    def process(slot):
        chunks = [slice(c * sub, (c + 1) * sub) for c in range(tk // sub)]
        in_group = lambda keys: slice(slot * tk + keys.start, slot * tk + keys.stop)
        m_all = m_s[...]
        l_all = l_s[...]
        bias_s[...] = bias_ref[0, 0, slot].astype(F32)

        def stage_a(hh, keys, mx):
            s = jnp.dot(k_ref[0, hh, in_group(keys), :], qt_ref[0, hh * HEAD_DIM:(hh + 1) * HEAD_DIM, :],
                        preferred_element_type=F32) + bias_s[keys, :]
            s_s[hh % 2, keys, :] = s
            return jnp.maximum(mx, jnp.max(s, axis=0, keepdims=True))

        def stage_b(hh, keys, m_new, lsum, pv):
            p = jnp.exp2(s_s[hh % 2, keys, :] - m_new)
            lsum = lsum + jnp.sum(p, axis=0, keepdims=True)
            pv = pv + jnp.dot(vt_ref[0, hh * HEAD_DIM:(hh + 1) * HEAD_DIM, in_group(keys)],
                              p.astype(MXU_DTYPE), preferred_element_type=F32)
            return lsum, pv

        m_new = m_all[0:1, :]
        for keys in chunks:
            m_new = stage_a(0, keys, m_new)
        m_out, l_out = [], []
        for hh in range(N_HEADS):
            rows = slice(hh * HEAD_DIM, (hh + 1) * HEAD_DIM)
            lsum = jnp.zeros((1, tq), F32)
            pv = jnp.zeros((HEAD_DIM, tq), F32)
            m_next = m_all[hh + 1:hh + 2, :] if hh + 1 < N_HEADS else None
            for keys in chunks:
                if m_next is not None:
                    m_next = stage_a(hh + 1, keys, m_next)
                lsum, pv = stage_b(hh, keys, m_new, lsum, pv)
            alpha = jnp.exp2(m_all[hh:hh + 1, :] - m_new)
            l_out.append(alpha * l_all[hh:hh + 1, :] + lsum)
            acc_s[rows, :] = alpha * acc_s[rows, :] + pv
            m_out.append(m_new)
            m_new = m_next
        m_s[...] = jnp.concatenate(m_out, axis=0)
        l_s[...] = jnp.concatenate(l_out, axis=0)

    process(0)
    for slot in range(1, kps):
        pl.when(group * kps + slot < nb)(functools.partial(process, slot))

    @pl.when(group == (nb - 1) // kps)
    def _():
        out_t = jnp.concatenate(
            [acc_s[hh * HEAD_DIM:(hh + 1) * HEAD_DIM, :] / l_s[hh:hh + 1, :] for hh in range(N_HEADS)], axis=0)
        o_ref[0] = out_t.T


def _attention(qt, k_hm, vt, bias, tq, tk, causal):
    b, width, t = qt.shape
    s_len = vt.shape[-1]
    nq, nkb = t // tq, s_len // tk
    kps = 2 if nkb % 2 == 0 else 1
    steps = [(i, g) for i in range(nq) for g in range(-(-_num_key_blocks(i, tq, tk, nkb, causal) // kps))]
    qtile = jnp.asarray([s[0] for s in steps], I32)
    kgroup = jnp.asarray([s[1] for s in steps], I32)
    grid_spec = pltpu.PrefetchScalarGridSpec(
        num_scalar_prefetch=2,
        grid=(b, len(steps)),
        in_specs=[pl.BlockSpec((1, width, tq), lambda bb, s, qi, kg: (bb, 0, qi[s])),
                  pl.BlockSpec((1, N_HEADS, kps * tk, HEAD_DIM), lambda bb, s, qi, kg: (bb, 0, kg[s], 0)),
                  pl.BlockSpec((1, width, kps * tk), lambda bb, s, qi, kg: (bb, 0, kg[s])),
                  pl.BlockSpec((1, 1, kps, tk, tq), lambda bb, s, qi, kg: (bb, qi[s], kg[s], 0, 0))],
        out_specs=pl.BlockSpec((1, tq, width), lambda bb, s, qi, kg: (bb, qi[s], 0)),
        scratch_shapes=[pltpu.VMEM((N_HEADS, tq), F32), pltpu.VMEM((N_HEADS, tq), F32),
                        pltpu.VMEM((width, tq), F32), pltpu.VMEM((2, tk, tq), F32),
                        pltpu.VMEM((tk, tq), F32)])
    return pl.pallas_call(
        functools.partial(_attn_kernel, tq=tq, tk=tk, nkb=nkb, causal=causal, kps=kps),
        grid_spec=grid_spec,
        out_shape=jax.ShapeDtypeStruct((b, t, width), F32),
        compiler_params=_cparams(("arbitrary", "arbitrary")),
        name="masked_attention",
    )(qtile, kgroup, qt, k_hm, vt, bias)


def _merge_kernel(x_ref, ys_ref, att_ref, za_ref, gs_ref, ga_ref, gate_ref, wps_ref, wpa_ref, wo_ref,
                  gfin_ref, o_ref, *, final_norm):
    ya = att_ref[0] * _silu(za_ref[0])
    merged = (jax.nn.sigmoid(gs_ref[0]) * _mxu_dot(ys_ref[...], wps_ref[...])
              + jax.nn.sigmoid(ga_ref[0]) * _mxu_dot(ya, wpa_ref[...]))
    x_new = x_ref[0] + gate_ref[0] * _mxu_dot(merged, wo_ref[...])
    if final_norm:
        x_new = x_new * lax.rsqrt(jnp.mean(x_new * x_new, axis=-1, keepdims=True) + EPS) * gfin_ref[...]
    o_ref[0] = x_new


def _merge(x, ys, att, za, gs, ga, gate, w_ps, w_pa, w_o, g_final, tm, final_norm):
    b, t, d = x.shape
    half = d // 2
    row = lambda width: pl.BlockSpec((1, tm, width), lambda i, j: (i, j, 0))
    const = lambda shape: pl.BlockSpec(shape, lambda i, j: (0,) * len(shape), pipeline_mode=pl.Buffered(1))
    return pl.pallas_call(
        functools.partial(_merge_kernel, final_norm=final_norm),
        grid=(b, t // tm),
        in_specs=[row(d), pl.BlockSpec((tm, half), lambda i, j: (j, i)), row(half), row(half), row(d), row(d),
                  pl.BlockSpec((1, 1, d), lambda i, j: (i, 0, 0)),
                  const((half, d)), const((half, d)), const((d, d)), const((1, d))],
        out_specs=row(d),
        out_shape=jax.ShapeDtypeStruct((b, t, d), F32),
        compiler_params=_cparams(("arbitrary", "arbitrary")),
        name="merge_out",
    )(x, ys, att, za, gs, ga, gate, w_ps.astype(MXU_DTYPE), w_pa.astype(MXU_DTYPE),
      w_o.astype(MXU_DTYPE), g_final.reshape(1, d))


def _row_tile(t, cap):
    tile = min(t, cap)
    assert t % tile == 0, (t, tile)
    return tile


def _key_blocks(s_len):
    s_pad = -(-s_len // LANES) * LANES
    for tk in (512, 384, 256, 128):
        if s_pad % tk == 0:
            return s_pad, tk
    raise AssertionError(s_len)


def _layer(x, mod, lw, ssm_par, h0_re, h0_im, past, final_norm):
    b, t, d = x.shape
    shift, scale, gate = (m.reshape(b, 1, d) for m in jnp.split(mod, 3, axis=-1))
    tm = _row_tile(t, PROJ_ROW_TILE)
    (u, zs, za, gs, ga, k, v, ki, wit, qt, qit, vt, k_hm) = _inproj(
        x, shift, scale, lw["g_norm"], lw["wmain"], lw["wsmall"], tm)

    wb, wc, coef = ssm_par
    ys, hr, hi = _ssm(u, zs, b, h0_re, h0_im, wb, wc, coef,
                      lw["d_skip"], lw["w_glu"], lw["b_glu"], min(t * b, PROJ_ROW_TILE))

    ki_bf = ki.astype(MXU_DTYPE)
    if past is None:
        causal, n_valid = True, t
        s_pad, tk = _key_blocks(n_valid)
        assert s_pad == n_valid
    else:
        past_k, past_v, past_ki = past
        plen = past_ki.shape[1]
        causal, n_valid = False, plen + t
        s_pad, tk = _key_blocks(n_valid)
        grow = lambda a, axis: jnp.pad(a.astype(MXU_DTYPE),
                                       [(0, s_pad - plen if ax == axis else 0) for ax in range(a.ndim)])
        k_hm = lax.dynamic_update_slice(grow(past_k.transpose(0, 2, 1, 3), 2), k_hm, (0, 0, plen, 0))
        vt = lax.dynamic_update_slice(grow(past_v.reshape(b, plen, d // 2).transpose(0, 2, 1), 2), vt,
                                      (0, 0, plen))
        ki_bf = lax.dynamic_update_slice(grow(past_ki, 1), ki_bf, (0, plen, 0))
    n_sel = min(TOPK_MAX, n_valid // 4)
    ki_blk = ki_bf.reshape(b, s_pad // tk, tk, IDX_DIM)

    tq = _row_tile(t, ROW_TILE)
    bias = _select(qit, wit, ki_blk, tq, causal, n_valid, n_sel)
    att = _attention(qt, k_hm, vt, bias, tq, tk, causal)

    x_new = _merge(x, ys, att, za, gs, ga, gate, lw["w_ps"], lw["w_pa"], lw["w_o"], lw["g_final"],
                   _row_tile(t, PROJ_ROW_TILE), final_norm)
    gshape = (b, -1, STATE)
    return x_new, (k.reshape(b, t, N_HEADS, HEAD_DIM), v.reshape(b, t, N_HEADS, HEAD_DIM), ki,
                   hr.reshape(gshape), hi.reshape(gshape))


def kernel(x_prompt, x_sample, cache_k, cache_v, cache_kidx, state_ssm_re, state_ssm_im, c_prompt, c_sample, w_mod, b_mod, g_norm, w_in, a_re, a_im, log_dt, b_re, b_im, c_re, c_im, d_skip, w_glu, b_glu, w_ps, w_pa, w_o, g_final):
    depth = w_mod.shape[0]
    bp, _, d = x_prompt.shape
    g, p = a_re.shape[1:]
    mod = _modulation(jnp.concatenate([c_prompt, c_sample], axis=0), w_mod, b_mod)
    zeros = jnp.zeros((bp, g * p), F32)
    xp, xs = x_prompt, x_sample
    outs_p, outs_s = [], []
    for l in range(depth):
        wmain, wsmall = _inproj_weights(w_in[l], d)
        lw = dict(g_norm=g_norm[l], wmain=wmain, wsmall=wsmall, d_skip=d_skip[l], w_glu=w_glu[l],
                  b_glu=b_glu[l], w_ps=w_ps[l], w_pa=w_pa[l], w_o=w_o[l], g_final=g_final)
        ssm_par = _ssm_prep(a_re[l], a_im[l], log_dt[l], b_re[l], b_im[l], c_re[l], c_im[l])
        last = l == depth - 1
        xp, op = _layer(xp, mod[l, :bp], lw, ssm_par, zeros, zeros, None, last)
        xs, os_ = _layer(xs, mod[l, bp:], lw, ssm_par, state_ssm_re[l].reshape(-1, g * p),
                         state_ssm_im[l].reshape(-1, g * p), (cache_k[l], cache_v[l], cache_kidx[l]), last)
        outs_p.append(op)
        outs_s.append(os_)
    stack = lambda outs: tuple(jnp.stack(z) for z in zip(*outs))
    return (xp, xs) + stack(outs_p) + stack(outs_s)
```

```python
import functools
import math

import jax
import jax.numpy as jnp
from jax import lax
from jax.experimental import pallas as pl
from jax.experimental.pallas import tpu as pltpu

F32 = jnp.float32
I32 = jnp.int32
MXU_DTYPE = jnp.bfloat16

CHUNK = 64
SSM_GROUP = 16
STATE = 64
N_HEADS = 8
HEAD_DIM = 64
IDX_HEADS = 8
IDX_DIM = 64
TOPK_MAX = 256
EPS = 1e-6
LOG2E = math.log2(math.e)

LANES = 128
SUBLANES = 8
VREGS = 64
ROW_TILE = 256
PROJ_ROW_TILE = 512
V7X_VMEM_BYTES = 64 * 1024 * 1024
VMEM_LIMIT = V7X_VMEM_BYTES - 8 * 1024 * 1024

INT_MIN = -(2 ** 31)
INT16_MIN = -(2 ** 15)
MASK_NEG = -1e30
M_INIT = -1e29
NO_TIE_LIMIT = 2 ** 30
COUNT_ROWS = 32


def _cparams(sem):
    return pltpu.CompilerParams(dimension_semantics=sem, vmem_limit_bytes=VMEM_LIMIT)


def _silu(x):
    return x * jax.nn.sigmoid(x)


def _mxu_dot(a, b):
    return jnp.dot(a.astype(MXU_DTYPE), b.astype(MXU_DTYPE), preferred_element_type=F32)


def _split_hi_lo(a):
    hi = a.astype(jnp.bfloat16)
    lo = (a - hi.astype(F32)).astype(jnp.bfloat16)
    return hi, lo


def _dot_3pass(a, b):
    ah, al = _split_hi_lo(a)
    bh, bl = _split_hi_lo(b)
    d = functools.partial(jnp.dot, preferred_element_type=F32)
    return d(ah, bh) + (d(ah, bl) + d(al, bh))


def _mod_kernel(c_ref, w_ref, b_ref, o_ref):
    o_ref[...] = _dot_3pass(_silu(c_ref[...]), w_ref[...]) + b_ref[...]


def _modulation(c_all, w_mod, b_mod):
    depth, d, d3 = w_mod.shape
    rows = c_all.shape[0]
    cols = d3 // 4
    return pl.pallas_call(
        _mod_kernel,
        grid=(depth, d3 // cols),
        in_specs=[
            pl.BlockSpec((rows, d), lambda l, j: (0, 0)),
            pl.BlockSpec((None, d, cols), lambda l, j: (l, 0, j)),
            pl.BlockSpec((None, 1, cols), lambda l, j: (l, 0, j)),
        ],
        out_specs=pl.BlockSpec((None, rows, cols), lambda l, j: (l, 0, j)),
        out_shape=jax.ShapeDtypeStruct((depth, rows, d3), F32),
        compiler_params=_cparams(("arbitrary", "arbitrary")),
        name="adaln_mod",
    )(c_all, w_mod, b_mod.reshape(depth, 1, d3))


def _ssm_prep_kernel(are_ref, aim_ref, ldt_ref, bre_ref, bim_ref, cre_ref, cim_ref,
                     wb_ref, wc_ref, coef_ref):
    a_re = are_ref[...]
    a_im = aim_ref[...]
    dt = jnp.exp(ldt_ref[...])
    gp = a_re.shape[-1]

    mag = jnp.exp(a_re * dt)
    ang = a_im * dt
    ab_re, ab_im = mag * jnp.cos(ang), mag * jnp.sin(ang)
    den = a_re * a_re + a_im * a_im
    nr = ab_re - 1.0
    ni = ab_im
    f_re = (nr * a_re + ni * a_im) / den
    f_im = (ni * a_re - nr * a_im) / den

    rows_b = lax.broadcasted_iota(I32, bre_ref.shape, 0) // SSM_GROUP
    cols_b = lax.broadcasted_iota(I32, bre_ref.shape, 1) // STATE
    on_b = rows_b == cols_b
    b_re = bre_ref[...]
    b_im = bim_ref[...]
    wb_ref[:, 0:gp] = jnp.where(on_b, f_re * b_re - f_im * b_im, 0.0).astype(wb_ref.dtype)
    wb_ref[:, gp:2 * gp] = jnp.where(on_b, f_re * b_im + f_im * b_re, 0.0).astype(wb_ref.dtype)

    rows_c = lax.broadcasted_iota(I32, cre_ref.shape, 0) // STATE
    cols_c = lax.broadcasted_iota(I32, cre_ref.shape, 1) // SSM_GROUP
    on_c = rows_c == cols_c
    wc_ref[0:gp, :] = jnp.where(on_c, cre_ref[...], 0.0).astype(wc_ref.dtype)
    wc_ref[gp:2 * gp, :] = jnp.where(on_c, -cim_ref[...], 0.0).astype(wc_ref.dtype)

    coef_ref[0] = jnp.broadcast_to(ab_re, (SUBLANES, gp))
    coef_ref[1] = jnp.broadcast_to(ab_im, (SUBLANES, gp))


def _ssm_prep(a_re, a_im, log_dt, b_re, b_im, c_re, c_im):
    g, p = a_re.shape
    n = b_re.shape[-1]
    gp, gn = g * p, g * n
    row = lambda a: a.reshape(1, gp)
    ldt = jnp.broadcast_to(log_dt[:, None], (g, p)).reshape(1, gp)
    b_exp = lambda b: jnp.broadcast_to(b.transpose(2, 0, 1).reshape(1, n, gp), (g, n, gp)).reshape(gn, gp)
    c_exp = lambda c: jnp.broadcast_to(c.transpose(0, 2, 1).reshape(gp, 1, n), (gp, g, n)).reshape(gp, gn)
    return pl.pallas_call(
        _ssm_prep_kernel,
        out_shape=(jax.ShapeDtypeStruct((gn, 2 * gp), MXU_DTYPE),
                   jax.ShapeDtypeStruct((2 * gp, gn), MXU_DTYPE),
                   jax.ShapeDtypeStruct((2, SUBLANES, gp), F32)),
        compiler_params=pltpu.CompilerParams(vmem_limit_bytes=VMEM_LIMIT),
        name="ssm_prep",
    )(row(a_re), row(a_im), ldt, b_exp(b_re), b_exp(b_im), c_exp(c_re), c_exp(c_im))


_MAIN_SEGS = ("u", "zs", "za", "gs", "ga", "k", "v", "q", "qi")


def _main_offsets(d):
    half = d // 2
    widths = dict(u=half, zs=half, za=half, gs=d, ga=d, k=half, v=half, q=half, qi=half)
    offs, o = {}, 0
    for name in _MAIN_SEGS:
        offs[name] = (o, o + widths[name])
        o += widths[name]
    return offs, o


def _inproj_kernel(x_ref, shift_ref, scale_ref, g_ref, wmain_ref, wsmall_ref,
                   u_ref, zs_ref, za_ref, gs_ref, ga_ref, k_ref, v_ref, ki_ref,
                   wit_ref, qt_ref, qit_ref, vt_ref, khm_ref, *, offs):
    x = x_ref[0]
    h = x * lax.rsqrt(jnp.mean(x * x, axis=-1, keepdims=True) + EPS) * g_ref[...]
    h = h * (1.0 + scale_ref[0]) + shift_ref[0]
    hb = h.astype(MXU_DTYPE)

    def proj(name):
        c0, c1 = offs[name]
        return jnp.dot(hb, wmain_ref[:, c0:c1], preferred_element_type=F32)

    u_ref[...] = proj("u")
    zs_ref[...] = proj("zs")
    za_ref[0] = proj("za")
    gs_ref[0] = proj("gs")
    ga_ref[0] = proj("ga")

    zk = proj("k")
    k_ref[0] = zk
    for hh in range(N_HEADS):
        khm_ref[0, hh] = zk[:, hh * HEAD_DIM:(hh + 1) * HEAD_DIM].astype(khm_ref.dtype)
    zv = proj("v")
    v_ref[0] = zv
    vt_ref[0] = zv.T.astype(vt_ref.dtype)
    qt_ref[0] = proj("q").T.astype(qt_ref.dtype)
    qit_ref[0] = proj("qi").T.astype(qit_ref.dtype)

    zsm = jnp.dot(hb, wsmall_ref[...], preferred_element_type=F32)
    ki_ref[0] = zsm[:, 0:IDX_DIM]
    wit_ref[0] = zsm.T[IDX_DIM:IDX_DIM + IDX_HEADS, :] * (IDX_HEADS ** -0.5)


def _inproj_weights(w_in, d):
    half = d // 2
    idx_w = IDX_HEADS * IDX_DIM
    o = 0
    seg = {}
    for name, width in (("u", half), ("zs", half), ("q", half), ("k", half), ("v", half), ("za", half),
                        ("qi", idx_w), ("ki", IDX_DIM), ("wi", IDX_HEADS), ("gs", d), ("ga", d)):
        seg[name] = w_in[:, o:o + width]
        o += width
    seg["q"] = seg["q"] * (HEAD_DIM ** -0.5 * LOG2E)
    seg["qi"] = seg["qi"] * (IDX_DIM ** -0.5)
    wmain = jnp.concatenate([seg[n] for n in _MAIN_SEGS], axis=1).astype(MXU_DTYPE)
    wsmall = jnp.concatenate(
        [seg["ki"], seg["wi"], jnp.zeros((d, LANES - IDX_DIM - IDX_HEADS), w_in.dtype)], axis=1).astype(MXU_DTYPE)
    return wmain, wsmall


def _inproj(x, shift, scale, g_norm, wmain, wsmall, tm):
    b, t, d = x.shape
    half = d // 2
    offs, wcols = _main_offsets(d)
    nt = t // tm
    row = lambda width: pl.BlockSpec((1, tm, width), lambda i, j: (i, j, 0))
    hm = pl.BlockSpec((1, N_HEADS, tm, HEAD_DIM), lambda i, j: (i, 0, j, 0))
    tr = lambda rows: pl.BlockSpec((1, rows, tm), lambda i, j: (i, 0, j))
    tb = pl.BlockSpec((tm, half), lambda i, j: (j, i))
    per_b = pl.BlockSpec((1, 1, d), lambda i, j: (i, 0, 0))
    const = lambda shape: pl.BlockSpec(shape, lambda i, j: (0,) * len(shape), pipeline_mode=pl.Buffered(1))
    f = lambda *s: jax.ShapeDtypeStruct(s, F32)
    bf = lambda *s: jax.ShapeDtypeStruct(s, MXU_DTYPE)
    return pl.pallas_call(
        functools.partial(_inproj_kernel, offs=offs),
        grid=(b, nt),
        in_specs=[row(d), per_b, per_b, const((1, d)), const((d, wcols)), const((d, LANES))],
        out_specs=[tb, tb, row(half), row(d), row(d), row(half), row(half),
                   row(IDX_DIM), tr(IDX_HEADS), tr(half), tr(half), tr(half), hm],
        out_shape=[f(t, b * half), f(t, b * half), f(b, t, half), f(b, t, d), f(b, t, d),
                   f(b, t, half), f(b, t, half), f(b, t, IDX_DIM), f(b, IDX_HEADS, t),
                   bf(b, half, t), bf(b, half, t), bf(b, half, t), bf(b, N_HEADS, t, HEAD_DIM)],
        compiler_params=_cparams(("arbitrary", "arbitrary")),
        name="inproj",
    )(x, shift, scale, g_norm.reshape(1, d), wmain, wsmall)


def _ssm_kernel(u_ref, zs_ref, h0r_ref, h0i_ref, wb_ref, wc_ref, coef_ref, d_ref, wglu_ref, bglu_ref,
                y_ref, hr_ref, hi_ref, bu_s, u_s, y_s, *, tt, nb, gp):
    @pl.when(pl.program_id(0) == 0)
    def _():
        hr_ref[...] = h0r_ref[...]
        hi_ref[...] = h0i_ref[...]

    slabs = u_s.shape[0]
    dn = slabs * LANES
    for bb in range(nb):
        for j in range(slabs):
            lanes = slice(bb * dn + j * LANES, bb * dn + (j + 1) * LANES)
            u_s[j, pl.ds(bb, tt, stride=nb), :] = u_ref[:, lanes]
    u = jnp.concatenate([u_s[j] for j in range(slabs)], axis=-1)
    ub = u.astype(MXU_DTYPE)
    sw = gp // slabs
    for j in range(slabs):
        ch = slice(j * LANES, (j + 1) * LANES)
        for part in range(2):
            st = slice(part * gp + j * sw, part * gp + (j + 1) * sw)
            bu_s[:, st] = jnp.dot(ub[:, ch], wb_ref[ch, st], preferred_element_type=F32)

    groups = nb // SUBLANES

    def step(t, carry):
        a_re = coef_ref[0]
        a_im = coef_ref[1]
        out = []
        for g in range(groups):
            h_re, h_im = carry[2 * g], carry[2 * g + 1]
            rows = pl.ds(pl.multiple_of(t * nb + g * SUBLANES, SUBLANES), SUBLANES)
            n_re = (a_re * h_re - a_im * h_im) + bu_s[rows, 0:gp]
            n_im = (a_re * h_im + a_im * h_re) + bu_s[rows, gp:2 * gp]
            bu_s[rows, 0:gp] = n_re
            bu_s[rows, gp:2 * gp] = n_im
            out += [n_re, n_im]
        return tuple(out)

    init = []
    for g in range(groups):
        rows = slice(g * SUBLANES, (g + 1) * SUBLANES)
        init += [hr_ref[rows, :], hi_ref[rows, :]]
    fin = lax.fori_loop(0, tt, step, tuple(init), unroll=2)
    for g in range(groups):
        rows = slice(g * SUBLANES, (g + 1) * SUBLANES)
        hr_ref[rows, :] = fin[2 * g]
        hi_ref[rows, :] = fin[2 * g + 1]

    ys = []
    for j in range(slabs):
        ch = slice(j * LANES, (j + 1) * LANES)
        acc = d_ref[:, ch] * u[:, ch]
        for part in range(2):
            st = slice(part * gp + j * sw, part * gp + (j + 1) * sw)
            acc = acc + jnp.dot(bu_s[:, st].astype(MXU_DTYPE), wc_ref[st, ch], preferred_element_type=F32)
        ys.append(acc)
    y = jax.nn.gelu(jnp.concatenate(ys, axis=-1))
    y = y * jax.nn.sigmoid(_mxu_dot(y, wglu_ref[...]) + bglu_ref[...])
    for j in range(slabs):
        y_s[j] = y[:, j * LANES:(j + 1) * LANES]
    for bb in range(nb):
        for j in range(slabs):
            lanes = slice(bb * dn + j * LANES, bb * dn + (j + 1) * LANES)
            y_ref[:, lanes] = y_s[j, pl.ds(bb, tt, stride=nb), :] * _silu(zs_ref[:, lanes])


def _ssm(u, zs, nb, h0_re, h0_im, wb, wc, coef, d_skip, w_glu, b_glu, rows_per_step):
    t, width = u.shape
    dn = width // nb
    gp = wb.shape[1] // 2
    assert nb % SUBLANES == 0 and rows_per_step % nb == 0
    tt = rows_per_step // nb
    assert t % tt == 0
    row = pl.BlockSpec((tt, width), lambda j: (j, 0))
    const = lambda shape: pl.BlockSpec(shape, lambda j: (0,) * len(shape))
    return pl.pallas_call(
        functools.partial(_ssm_kernel, tt=tt, nb=nb, gp=gp),
        grid=(t // tt,),
        in_specs=[row, row, const((nb, gp)), const((nb, gp)), const(wb.shape), const(wc.shape),
                  const(coef.shape), const((1, dn)), const(w_glu.shape), const((1, dn))],
        out_specs=[row, const((nb, gp)), const((nb, gp))],
        out_shape=[jax.ShapeDtypeStruct((t, width), F32),
                   jax.ShapeDtypeStruct((nb, gp), F32),
                   jax.ShapeDtypeStruct((nb, gp), F32)],
        scratch_shapes=[pltpu.VMEM((rows_per_step, 2 * gp), F32),
                        pltpu.VMEM((dn // LANES, rows_per_step, LANES), F32),
                        pltpu.VMEM((dn // LANES, rows_per_step, LANES), F32)],
        compiler_params=_cparams(("arbitrary",)),
        name="ssm_scan",
    )(u, zs, h0_re, h0_im, wb, wc, coef,
      d_skip.reshape(1, dn), w_glu.astype(MXU_DTYPE), b_glu.reshape(1, dn))


def _num_key_blocks(i, tq, tk, nkb, causal):
    if not causal:
        return nkb
    return ((i + 1) * tq + (tk - 1)) // tk


def _key_chunk(tq, tk):
    lane_tiles = -(-tq // LANES)
    for sub in (256, 128, 64, 32, 16, 8):
        if tk % sub == 0 and (sub // SUBLANES) * lane_tiles <= VREGS // 2:
            return sub
    raise AssertionError((tq, tk))


def _select_kernel(qit_ref, wit_ref, ki_ref, bias_ref, keys_s, hi_s, lo_s, lg_s, *,
                   tq, tk, nkb, causal, n_valid, n_sel):
    i = pl.program_id(1)
    nb = _num_key_blocks(i, tq, tk, nkb, causal)
    if causal:
        q_t = i * tq + lax.broadcasted_iota(I32, (1, tq), 1)
        limit = (q_t // CHUNK + 1) * CHUNK
    else:
        limit = n_valid
    wit = wit_ref[0]

    def key_index(j):
        return j * tk + lax.broadcasted_iota(I32, (tk, tq), 0)

    sub = _key_chunk(tq, tk)

    def score_block(j, masked):
        for cc in range(tk // sub):
            rows = slice(cc * sub, (cc + 1) * sub)
            kb = ki_ref[0, j, rows, :]

            def logits(hh):
                return jnp.dot(kb, qit_ref[0, hh * IDX_DIM:(hh + 1) * IDX_DIM, :], preferred_element_type=F32)

            lg_s[0] = logits(0)
            s = jnp.zeros((sub, tq), F32)
            for hh in range(IDX_HEADS):
                if hh + 1 < IDX_HEADS:
                    lg_s[(hh + 1) % 2] = logits(hh + 1)
                s = s + wit[hh:hh + 1, :] * jnp.maximum(lg_s[hh % 2], 0.0)
            bits = pltpu.bitcast(s, I32)
            sign = bits >> 31
            key = (bits ^ (sign & 0x7FFFFFFF)) - sign
            if masked:
                k_idx = j * tk + cc * sub + lax.broadcasted_iota(I32, (sub, tq), 0)
                key = jnp.where(k_idx < limit, key, INT_MIN)
            keys_s[j, rows, :] = key
            hi_s[j, rows, :] = (key >> 16).astype(jnp.int16)

    n_full = ((i * tq) // CHUNK + 1) * CHUNK // tk if causal else n_valid // tk

    def full_block(j, c):
        score_block(j, False)
        return c

    def edge_block(j, c):
        score_block(j, True)
        return c

    lax.fori_loop(0, n_full, full_block, 0)
    lax.fori_loop(n_full, nb, edge_block, 0)

    def count(pred_fns):
        def body(j, accs):
            key = keys_s[j]
            return tuple(acc + jnp.sum(jnp.where(fn(key), 1.0, 0.0).reshape(tk // COUNT_ROWS, COUNT_ROWS, tq),
                                       axis=0)
                         for fn, acc in zip(pred_fns, accs))
        accs = lax.fori_loop(0, nb, body, tuple(jnp.zeros((COUNT_ROWS, tq), F32) for _ in pred_fns))
        return [jnp.sum(a, axis=0, keepdims=True) for a in accs]

    def count16(ref, thr16, strict):
        def body(j, acc):
            blk = ref[j]
            hit = blk > thr16 if strict else blk >= thr16
            ind = jnp.where(hit, jnp.int16(1), jnp.int16(0))
            part = ind[0:COUNT_ROWS]
            for r in range(1, tk // COUNT_ROWS):
                part = part + ind[r * COUNT_ROWS:(r + 1) * COUNT_ROWS]
            return acc + part.astype(F32)
        acc = lax.fori_loop(0, nb, body, jnp.zeros((COUNT_ROWS, tq), F32))
        return jnp.sum(acc, axis=0, keepdims=True)

    def radix16(ref, need, count_all):
        def bit_step(bi, carry):
            u, c_acc = carry
            cand = u | lax.shift_left(jnp.int32(1), 15 - bi)
            cnt = count16(ref, (cand + INT16_MIN).astype(jnp.int16), False)
            keep = cnt >= need
            return jnp.where(keep, cand, u), jnp.where(keep, cnt, c_acc)
        u, c_acc = lax.fori_loop(0, 16, bit_step, (jnp.zeros((1, tq), I32), count_all))
        return u + INT16_MIN, c_acc

    n_adm = jnp.broadcast_to(jnp.asarray(limit, F32), (1, tq))
    n_all = jnp.broadcast_to((nb * tk).astype(F32) if causal else jnp.float32(nb * tk), (1, tq))
    hi_sel, c_ge_hi = radix16(hi_s, n_sel, n_all)
    c_gt_hi = count16(hi_s, hi_sel.astype(jnp.int16), True)
    want_lo = n_sel - c_gt_hi

    def low_block(j, c):
        key = keys_s[j]
        low = (key & 0xFFFF) + INT16_MIN
        lo_s[j] = jnp.where((key >> 16) == hi_sel, low, INT16_MIN).astype(jnp.int16)
        return c

    lax.fori_loop(0, nb, low_block, 0)
    lo_sel, c_ge_lo = radix16(lo_s, want_lo, c_ge_hi - c_gt_hi)
    kth = lax.shift_left(hi_sel, 16) | (lo_sel - INT16_MIN)
    c_ge = jnp.where(kth == INT_MIN, n_adm, c_gt_hi + c_ge_lo)
    thr = jnp.maximum(kth, INT_MIN + 1)
    tie_q = c_ge > n_sel
    has_tie = jnp.max(jnp.where(tie_q, 1.0, 0.0)) > 0.0

    @pl.when(jnp.logical_not(has_tie))
    def _():
        def bias_block(j, c):
            bias_ref[0, 0, j] = jnp.where(keys_s[j] >= thr, 0.0, MASK_NEG).astype(bias_ref.dtype)
            return c

        lax.fori_loop(0, nb, bias_block, 0)

    @pl.when(has_tie)
    def _():
        (c_gt,) = count([lambda kk: kk > thr])
        want = n_sel - c_gt
        tri = jnp.where(lax.broadcasted_iota(I32, (tk, tk), 0) >= lax.broadcasted_iota(I32, (tk, tk), 1),
                        1.0, 0.0).astype(jnp.bfloat16)

        def body(j, carry):
            before, below = carry
            eq = jnp.where(keys_s[j] == thr, 1.0, 0.0).astype(jnp.bfloat16)
            pref = before + jnp.dot(tri, eq, preferred_element_type=F32)
            below = below + jnp.sum(jnp.where(pref < want, 1.0, 0.0), axis=0, keepdims=True)
            return pref[tk - 1:tk, :], below

        _, below = lax.fori_loop(0, nb, body, (jnp.zeros((1, tq), F32), jnp.zeros((1, tq), F32)))
        jstar = jnp.where(tie_q, below.astype(I32), NO_TIE_LIMIT)

        def bias_block(j, c):
            key = keys_s[j]
            tie_ok = jnp.where(key_index(j) <= jstar, 0.0, MASK_NEG)
            bias = jnp.where(key > thr, 0.0, jnp.where(key == thr, tie_ok, MASK_NEG))
            bias_ref[0, 0, j] = bias.astype(bias_ref.dtype)
            return c

        lax.fori_loop(0, nb, bias_block, 0)

    def masked_block(j, c):
        bias_ref[0, 0, j] = jnp.full((tk, tq), MASK_NEG, bias_ref.dtype)
        return c

    lax.fori_loop(nb, nkb, masked_block, 0)


def _select(qit, wit, ki_blk, tq, causal, n_valid, n_sel):
    b, width, t = qit.shape
    _, nkb, tk, _ = ki_blk.shape
    nq = t // tq
    return pl.pallas_call(
        functools.partial(_select_kernel, tq=tq, tk=tk, nkb=nkb, causal=causal, n_valid=n_valid, n_sel=n_sel),
        grid=(b, nq),
        in_specs=[pl.BlockSpec((1, width, tq), lambda i, j: (i, 0, j)),
                  pl.BlockSpec((1, IDX_HEADS, tq), lambda i, j: (i, 0, j)),
                  pl.BlockSpec((1, nkb, tk, IDX_DIM), lambda i, j: (i, 0, 0, 0))],
        out_specs=pl.BlockSpec((1, 1, nkb, tk, tq), lambda i, j: (i, j, 0, 0, 0)),
        out_shape=jax.ShapeDtypeStruct((b, nq, nkb, tk, tq), jnp.bfloat16),
        scratch_shapes=[pltpu.VMEM((nkb, tk, tq), I32), pltpu.VMEM((nkb, tk, tq), jnp.int16),
                        pltpu.VMEM((nkb, tk, tq), jnp.int16), pltpu.VMEM((2, _key_chunk(tq, tk), tq), F32)],
        compiler_params=_cparams(("arbitrary", "arbitrary")),
        name="topk_select",
    )(qit, wit, ki_blk)


def _attn_kernel(qtile_ref, kgroup_ref, qt_ref, k_ref, vt_ref, bias_ref, o_ref, m_s, l_s, acc_s, s_s, bias_s, *,
                 tq, tk, nkb, causal, kps):
    step = pl.program_id(1)
    group = kgroup_ref[step]
    nb = _num_key_blocks(qtile_ref[step], tq, tk, nkb, causal)

    @pl.when(group == 0)
    def _():
        m_s[...] = jnp.full(m_s.shape, M_INIT, F32)
        l_s[...] = jnp.zeros(l_s.shape, F32)
        acc_s[...] = jnp.zeros(acc_s.shape, F32)

    sub = _key_chunk(tq, tk)

    def process(slot):
        chunks = [slice(c * sub, (c + 1) * sub) for c in range(tk // sub)]
        in_group = lambda keys: slice(slot * tk + keys.start, slot * tk + keys.stop)
        m_all = m_s[...]
        l_all = l_s[...]
        bias_s[...] = bias_ref[0, 0, slot].astype(F32)

        def stage_a(hh, keys, mx):
            s = jnp.dot(k_ref[0, hh, in_group(keys), :], qt_ref[0, hh * HEAD_DIM:(hh + 1) * HEAD_DIM, :],
                        preferred_element_type=F32) + bias_s[keys, :]
            s_s[hh % 2, keys, :] = s
            return jnp.maximum(mx, jnp.max(s, axis=0, keepdims=True))

        def stage_b(hh, keys, m_new, lsum, pv):
            p = jnp.exp2(s_s[hh % 2, keys, :] - m_new)
            lsum = lsum + jnp.sum(p, axis=0, keepdims=True)
            pv = pv + jnp.dot(vt_ref[0, hh * HEAD_DIM:(hh + 1) * HEAD_DIM, in_group(keys)],
                              p.astype(MXU_DTYPE), preferred_element_type=F32)
            return lsum, pv

        m_new = m_all[0:1, :]
        for keys in chunks:
            m_new = stage_a(0, keys, m_new)
        m_out, l_out = [], []
        for hh in range(N_HEADS):
            rows = slice(hh * HEAD_DIM, (hh + 1) * HEAD_DIM)
            lsum = jnp.zeros((1, tq), F32)
            pv = jnp.zeros((HEAD_DIM, tq), F32)
            m_next = m_all[hh + 1:hh + 2, :] if hh + 1 < N_HEADS else None
            for keys in chunks:
                if m_next is not None:
                    m_next = stage_a(hh + 1, keys, m_next)
                lsum, pv = stage_b(hh, keys, m_new, lsum, pv)
            alpha = jnp.exp2(m_all[hh:hh + 1, :] - m_new)
            l_out.append(alpha * l_all[hh:hh + 1, :] + lsum)
            acc_s[rows, :] = alpha * acc_s[rows, :] + pv
            m_out.append(m_new)
            m_new = m_next
        m_s[...] = jnp.concatenate(m_out, axis=0)
        l_s[...] = jnp.concatenate(l_out, axis=0)

    process(0)
    for slot in range(1, kps):
        pl.when(group * kps + slot < nb)(functools.partial(process, slot))

    @pl.when(group == (nb - 1) // kps)
    def _():
        out_t = jnp.concatenate(
            [acc_s[hh * HEAD_DIM:(hh + 1) * HEAD_DIM, :] / l_s[hh:hh + 1, :] for hh in range(N_HEADS)], axis=0)
        o_ref[0] = out_t.T


def _attention(qt, k_hm, vt, bias, tq, tk, causal):
    b, width, t = qt.shape
    s_len = vt.shape[-1]
    nq, nkb = t // tq, s_len // tk
    kps = next(n for n in (4, 2, 1) if nkb % n == 0)
    steps = [(i, g) for i in range(nq) for g in range(-(-_num_key_blocks(i, tq, tk, nkb, causal) // kps))]
    qtile = jnp.asarray([s[0] for s in steps], I32)
    kgroup = jnp.asarray([s[1] for s in steps], I32)
    grid_spec = pltpu.PrefetchScalarGridSpec(
        num_scalar_prefetch=2,
        grid=(b, len(steps)),
        in_specs=[pl.BlockSpec((1, width, tq), lambda bb, s, qi, kg: (bb, 0, qi[s])),
                  pl.BlockSpec((1, N_HEADS, kps * tk, HEAD_DIM), lambda bb, s, qi, kg: (bb, 0, kg[s], 0)),
                  pl.BlockSpec((1, width, kps * tk), lambda bb, s, qi, kg: (bb, 0, kg[s])),
                  pl.BlockSpec((1, 1, kps, tk, tq), lambda bb, s, qi, kg: (bb, qi[s], kg[s], 0, 0))],
        out_specs=pl.BlockSpec((1, tq, width), lambda bb, s, qi, kg: (bb, qi[s], 0)),
        scratch_shapes=[pltpu.VMEM((N_HEADS, tq), F32), pltpu.VMEM((N_HEADS, tq), F32),
                        pltpu.VMEM((width, tq), F32), pltpu.VMEM((2, tk, tq), F32),
                        pltpu.VMEM((tk, tq), F32)])
    return pl.pallas_call(
        functools.partial(_attn_kernel, tq=tq, tk=tk, nkb=nkb, causal=causal, kps=kps),
        grid_spec=grid_spec,
        out_shape=jax.ShapeDtypeStruct((b, t, width), F32),
        compiler_params=_cparams(("arbitrary", "arbitrary")),
        name="masked_attention",
    )(qtile, kgroup, qt, k_hm, vt, bias)


def _merge_kernel(x_ref, ys_ref, att_ref, za_ref, gs_ref, ga_ref, gate_ref, wps_ref, wpa_ref, wo_ref,
                  gfin_ref, o_ref, *, final_norm):
    ya = att_ref[0] * _silu(za_ref[0])
    merged = (jax.nn.sigmoid(gs_ref[0]) * _mxu_dot(ys_ref[...], wps_ref[...])
              + jax.nn.sigmoid(ga_ref[0]) * _mxu_dot(ya, wpa_ref[...]))
    x_new = x_ref[0] + gate_ref[0] * _mxu_dot(merged, wo_ref[...])
    if final_norm:
        x_new = x_new * lax.rsqrt(jnp.mean(x_new * x_new, axis=-1, keepdims=True) + EPS) * gfin_ref[...]
    o_ref[0] = x_new


def _merge(x, ys, att, za, gs, ga, gate, w_ps, w_pa, w_o, g_final, tm, final_norm):
    b, t, d = x.shape
    half = d // 2
    row = lambda width: pl.BlockSpec((1, tm, width), lambda i, j: (i, j, 0))
    const = lambda shape: pl.BlockSpec(shape, lambda i, j: (0,) * len(shape), pipeline_mode=pl.Buffered(1))
    return pl.pallas_call(
        functools.partial(_merge_kernel, final_norm=final_norm),
        grid=(b, t // tm),
        in_specs=[row(d), pl.BlockSpec((tm, half), lambda i, j: (j, i)), row(half), row(half), row(d), row(d),
                  pl.BlockSpec((1, 1, d), lambda i, j: (i, 0, 0)),
                  const((half, d)), const((half, d)), const((d, d)), const((1, d))],
        out_specs=row(d),
        out_shape=jax.ShapeDtypeStruct((b, t, d), F32),
        compiler_params=_cparams(("arbitrary", "arbitrary")),
        name="merge_out",
    )(x, ys, att, za, gs, ga, gate, w_ps.astype(MXU_DTYPE), w_pa.astype(MXU_DTYPE),
      w_o.astype(MXU_DTYPE), g_final.reshape(1, d))


def _row_tile(t, cap):
    tile = min(t, cap)
    assert t % tile == 0, (t, tile)
    return tile


def _key_blocks(s_len):
    s_pad = -(-s_len // LANES) * LANES
    for tk in (512, 384, 256, 128):
        if s_pad % tk == 0:
            return s_pad, tk
    raise AssertionError(s_len)


def _layer(x, mod, lw, ssm_par, h0_re, h0_im, past, final_norm):
    b, t, d = x.shape
    shift, scale, gate = (m.reshape(b, 1, d) for m in jnp.split(mod, 3, axis=-1))
    tm = _row_tile(t, PROJ_ROW_TILE)
    (u, zs, za, gs, ga, k, v, ki, wit, qt, qit, vt, k_hm) = _inproj(
        x, shift, scale, lw["g_norm"], lw["wmain"], lw["wsmall"], tm)

    wb, wc, coef = ssm_par
    ys, hr, hi = _ssm(u, zs, b, h0_re, h0_im, wb, wc, coef,
                      lw["d_skip"], lw["w_glu"], lw["b_glu"], min(t * b, PROJ_ROW_TILE))

    ki_bf = ki.astype(MXU_DTYPE)
    if past is None:
        causal, n_valid = True, t
        s_pad, tk = _key_blocks(n_valid)
        assert s_pad == n_valid
    else:
        past_k, past_v, past_ki = past
        plen = past_ki.shape[1]
        causal, n_valid = False, plen + t
        s_pad, tk = _key_blocks(n_valid)
        grow = lambda a, axis: jnp.pad(a.astype(MXU_DTYPE),
                                       [(0, s_pad - plen if ax == axis else 0) for ax in range(a.ndim)])
        k_hm = lax.dynamic_update_slice(grow(past_k.transpose(0, 2, 1, 3), 2), k_hm, (0, 0, plen, 0))
        vt = lax.dynamic_update_slice(grow(past_v.reshape(b, plen, d // 2).transpose(0, 2, 1), 2), vt,
                                      (0, 0, plen))
        ki_bf = lax.dynamic_update_slice(grow(past_ki, 1), ki_bf, (0, plen, 0))
    n_sel = min(TOPK_MAX, n_valid // 4)
    ki_blk = ki_bf.reshape(b, s_pad // tk, tk, IDX_DIM)

    tq = _row_tile(t, ROW_TILE)
    bias = _select(qit, wit, ki_blk, tq, causal, n_valid, n_sel)
    att = _attention(qt, k_hm, vt, bias, tq, tk, causal)

    x_new = _merge(x, ys, att, za, gs, ga, gate, lw["w_ps"], lw["w_pa"], lw["w_o"], lw["g_final"],
                   _row_tile(t, PROJ_ROW_TILE), final_norm)
    gshape = (b, -1, STATE)
    return x_new, (k.reshape(b, t, N_HEADS, HEAD_DIM), v.reshape(b, t, N_HEADS, HEAD_DIM), ki,
                   hr.reshape(gshape), hi.reshape(gshape))


def kernel(x_prompt, x_sample, cache_k, cache_v, cache_kidx, state_ssm_re, state_ssm_im, c_prompt, c_sample, w_mod, b_mod, g_norm, w_in, a_re, a_im, log_dt, b_re, b_im, c_re, c_im, d_skip, w_glu, b_glu, w_ps, w_pa, w_o, g_final):
    depth = w_mod.shape[0]
    bp, _, d = x_prompt.shape
    g, p = a_re.shape[1:]
    mod = _modulation(jnp.concatenate([c_prompt, c_sample], axis=0), w_mod, b_mod)
    zeros = jnp.zeros((bp, g * p), F32)
    xp, xs = x_prompt, x_sample
    outs_p, outs_s = [], []
    for l in range(depth):
        wmain, wsmall = _inproj_weights(w_in[l], d)
        lw = dict(g_norm=g_norm[l], wmain=wmain, wsmall=wsmall, d_skip=d_skip[l], w_glu=w_glu[l],
                  b_glu=b_glu[l], w_ps=w_ps[l], w_pa=w_pa[l], w_o=w_o[l], g_final=g_final)
        ssm_par = _ssm_prep(a_re[l], a_im[l], log_dt[l], b_re[l], b_im[l], c_re[l], c_im[l])
        last = l == depth - 1
        xp, op = _layer(xp, mod[l, :bp], lw, ssm_par, zeros, zeros, None, last)
        xs, os_ = _layer(xs, mod[l, bp:], lw, ssm_par, state_ssm_re[l].reshape(-1, g * p),
                         state_ssm_im[l].reshape(-1, g * p), (cache_k[l], cache_v[l], cache_kidx[l]), last)
        outs_p.append(op)
        outs_s.append(os_)
    stack = lambda outs: tuple(jnp.stack(z) for z in zip(*outs))
    return (xp, xs) + stack(outs_p) + stack(outs_s)
```
